```python
import functools
import jax, jax.numpy as jnp
from jax import lax
import numpy as np

D_MODEL = 4096
BATCH = 4
SEQ = 2048
DEPTH = 2
DEC_BATCH = 128
DEC_SEQ = 1
PAST_LEN = 16384
PAGE_SIZE = 128

D_MIX = D_MODEL
RET_HEADS = 6
RET_DK = 128
RET_DV = 256
RET_CHUNK = 128
MLA_HEADS = 12
MLA_Q_LORA = 768
MLA_KV_LORA = 512
MLA_NOPE = 128
MLA_ROPE = 64
MLA_V = 128
MLA_SCALE = (MLA_NOPE + MLA_ROPE) ** -0.5
ATTN_BLOCK = 128
CONV_CH = D_MIX - RET_HEADS * RET_DV - MLA_HEADS * MLA_V
CONV_WIDTH = 31
CONV_GROUPS = 8
N_GROUPS = 4
EXPERTS_PER_GROUP = 8
N_EXPERTS = N_GROUPS * EXPERTS_PER_GROUP
TOP_K_IN_GROUP = 2
D_EXPERT = 256
ROPE_BASE = 10000.0
NORM_EPS = 1e-5
NEG_INF = -1e30
DEEPNORM_ALPHA = (2 * DEPTH) ** 0.25
DEEPNORM_BETA = (8 * DEPTH) ** -0.25
IN_SIZES = (RET_HEADS * RET_DK, RET_HEADS * RET_DK, RET_HEADS * RET_DV, RET_HEADS * RET_DV, MLA_Q_LORA, MLA_KV_LORA, MLA_ROPE, 2 * CONV_CH)
D_IN = sum(IN_SIZES)

kernel_name = 'hybrid_ret_mla_conv_hmoe_step'


def rms_norm(x, g):
    xf = x.astype(jnp.float32)
    y = xf * lax.rsqrt(jnp.mean(xf * xf, -1, keepdims=True) + NORM_EPS)
    return (y * g.astype(jnp.float32)).astype(x.dtype)


def layer_norm(x, g, b):
    xf = x.astype(jnp.float32)
    xc = xf - jnp.mean(xf, -1, keepdims=True)
    var = jnp.mean(xc * xc, -1, keepdims=True)
    return (xc * lax.rsqrt(var + NORM_EPS) * g.astype(jnp.float32) + b.astype(jnp.float32)).astype(x.dtype)


def rope(x, pos):
    d = x.shape[-1]
    inv = ROPE_BASE ** (-jnp.arange(0, d, 2, dtype=jnp.float32) / d)
    ang = pos.astype(jnp.float32)[:, None] * inv[None, :]
    shape = (pos.shape[0],) + (1,) * (x.ndim - 3) + (d // 2,)
    cos = jnp.cos(ang).reshape(shape)
    sin = jnp.sin(ang).reshape(shape)
    xf = x.astype(jnp.float32)
    x1, x2 = xf[..., : d // 2], xf[..., d // 2:]
    return jnp.concatenate([x1 * cos - x2 * sin, x2 * cos + x1 * sin], -1).astype(x.dtype)


def split_in(h):
    offs = [int(o) for o in np.cumsum(IN_SIZES)[:-1]]
    return jnp.split(h, offs, axis=-1)


def retention_chunk(S, q, k, v, log_g):
    C = q.shape[1]
    idx = jnp.arange(C, dtype=jnp.float32)
    diff = idx[:, None] - idx[None, :]
    decay = jnp.where(diff >= 0, jnp.exp(log_g[:, None, None] * jnp.maximum(diff, 0.0)[None]), 0.0)
    scores = jnp.einsum('bihd,bjhd->bhij', q, k) * decay
    o = jnp.einsum('bhij,bjhv->bihv', scores, v)
    cross = jnp.exp((idx[:, None] + 1.0) * log_g[None, :])
    o = o + jnp.einsum('bihd,bhdv->bihv', q, S) * cross[None, :, :, None]
    k_dec = k * jnp.exp((C - 1.0 - idx)[:, None] * log_g[None, :])[None, :, :, None]
    S_new = jnp.exp(C * log_g)[None, :, None, None] * S + jnp.einsum('bjhd,bjhv->bhdv', k_dec, v)
    return o, S_new


def retention_mixer(rq, rk, rv, rg, pos, S0, gn_g, gn_b):
    B, T, _ = rq.shape
    q = rope(rq.reshape(B, T, RET_HEADS, RET_DK), pos).astype(jnp.float32)
    k = rope(rk.reshape(B, T, RET_HEADS, RET_DK), pos).astype(jnp.float32) * (RET_DK ** -0.5)
    v = rv.reshape(B, T, RET_HEADS, RET_DV).astype(jnp.float32)
    chunk = RET_CHUNK if T % RET_CHUNK == 0 else T
    n = T // chunk
    log_g = jnp.log1p(-jnp.exp2(-5.0 - jnp.arange(RET_HEADS, dtype=jnp.float32)))

    def to_chunks(a):
        return a.reshape((B, n, chunk) + a.shape[2:]).swapaxes(0, 1)

    def step(S, qkv):
        o, S_new = retention_chunk(S, qkv[0], qkv[1], qkv[2], log_g)
        return S_new, o

    S_fin, o = lax.scan(step, S0.astype(jnp.float32), (to_chunks(q), to_chunks(k), to_chunks(v)))
    o = o.swapaxes(0, 1).reshape(B, T, RET_HEADS, RET_DV)
    o = layer_norm(o, gn_g.reshape(RET_HEADS, RET_DV), gn_b.reshape(RET_HEADS, RET_DV))
    out = jax.nn.silu(rg.astype(jnp.float32)) * o.reshape(B, T, RET_HEADS * RET_DV)
    return out.astype(rq.dtype), S_fin


def mla_project(cq, ckv, kr_lin, pos, q_norm_g, kv_norm_g, w_uq):
    q = jnp.einsum('btc,chd->bthd', rms_norm(cq, q_norm_g), w_uq)
    q_nope = q[..., :MLA_NOPE]
    q_rope = rope(q[..., MLA_NOPE:], pos)
    latent = rms_norm(ckv, kv_norm_g)
    k_rope = rope(kr_lin, pos)
    return q_nope, q_rope, latent, k_rope


def mla_prompt_attention(q_nope, q_rope, latent, k_rope, w_uk, w_uv):
    B, S = latent.shape[0], latent.shape[1]
    k_nope = jnp.einsum('bsc,chd->bshd', latent, w_uk)
    v = jnp.einsum('bsc,chd->bshd', latent, w_uv)
    nb = S // ATTN_BLOCK
    qn_b = q_nope.reshape(B, nb, ATTN_BLOCK, MLA_HEADS, MLA_NOPE).swapaxes(0, 1)
    qr_b = q_rope.reshape(B, nb, ATTN_BLOCK, MLA_HEADS, MLA_ROPE).swapaxes(0, 1)
    starts = jnp.arange(nb, dtype=jnp.int32) * ATTN_BLOCK
    k_pos = jnp.arange(S, dtype=jnp.int32)

    def block(args):
        qn, qr, q0 = args
        s = (jnp.einsum('bqhd,bkhd->bhqk', qn, k_nope).astype(jnp.float32)
             + jnp.einsum('bqhr,bkr->bhqk', qr, k_rope).astype(jnp.float32)) * MLA_SCALE
        q_pos = q0 + jnp.arange(ATTN_BLOCK, dtype=jnp.int32)
        s = jnp.where(k_pos[None, :] <= q_pos[:, None], s, NEG_INF)
        p = jax.nn.softmax(s, axis=-1).astype(v.dtype)
        return jnp.einsum('bhqk,bkhd->bqhd', p, v)

    o = lax.map(block, (qn_b, qr_b, starts))
    return o.swapaxes(0, 1).reshape(B, S, MLA_HEADS * MLA_V)


def mla_sample_attention(q_nope, q_rope, latent, k_rope, w_uk, w_uv, cache_kv, cache_kr, layer, page_table):
    DB, T = latent.shape[0], latent.shape[1]
    q_lat = jnp.einsum('bthd,chd->bthc', q_nope, w_uk).astype(jnp.float32) * MLA_SCALE
    q_r = q_rope.astype(jnp.float32) * MLA_SCALE

    def update(carry, kv, kr, mask):
        m, l, acc = carry
        kvf = kv.astype(jnp.float32)
        s = jnp.einsum('bthc,bpc->bthp', q_lat, kvf) + jnp.einsum('bthr,bpr->bthp', q_r, kr.astype(jnp.float32))
        if mask is not None:
            s = jnp.where(mask, s, NEG_INF)
        m_new = jnp.maximum(m, jnp.max(s, -1))
        corr = jnp.exp(m - m_new)
        p = jnp.exp(s - m_new[..., None])
        l = l * corr + jnp.sum(p, -1)
        acc = acc * corr[..., None] + jnp.einsum('bthp,bpc->bthc', p, kvf)
        return (m_new, l, acc)

    init = (jnp.full((DB, T, MLA_HEADS), NEG_INF, jnp.float32),
            jnp.zeros((DB, T, MLA_HEADS), jnp.float32),
            jnp.zeros((DB, T, MLA_HEADS, MLA_KV_LORA), jnp.float32))

    def page_step(carry, phys):
        kv = cache_kv[layer, phys]
        kr = cache_kr[layer, phys]
        return update(carry, kv, kr, None), None

    carry, _ = lax.scan(page_step, init, page_table.T)
    t_idx = jnp.arange(T)
    mask = (t_idx[None, :] <= t_idx[:, None])[None, :, None, :]
    m, l, acc = update(carry, latent, k_rope, mask)
    ctx = (acc / l[..., None]).astype(latent.dtype)
    o = jnp.einsum('bthc,chd->bthd', ctx, w_uv)
    return o.reshape(DB, T, MLA_HEADS * MLA_V)


def conv_mixer(pw, buf, w_dw, b_dw, gn_g, gn_b):
    B, T, _ = pw.shape
    a, gate = pw[..., :CONV_CH], pw[..., CONV_CH:]
    u = a * jax.nn.sigmoid(gate)
    full = jnp.concatenate([buf.astype(u.dtype), u], axis=1)
    y = lax.conv_general_dilated(full, w_dw[:, None, :].astype(u.dtype), window_strides=(1,), padding='VALID',
                                 dimension_numbers=('NWC', 'WIO', 'NWC'), feature_group_count=CONV_CH)
    y = y + b_dw
    gsz = CONV_CH // CONV_GROUPS
    y = layer_norm(y.reshape(B, T, CONV_GROUPS, gsz), gn_g.reshape(CONV_GROUPS, gsz), gn_b.reshape(CONV_GROUPS, gsz))
    y = jax.nn.silu(y.reshape(B, T, CONV_CH))
    new_buf = full[:, full.shape[1] - (CONV_WIDTH - 1):]
    return y, new_buf


def hier_moe(x, w_gr, b_gr, w_er, b_er, w1, w3, w2):
    B, T, D = x.shape
    xt = x.reshape(B * T, D)
    N = B * T
    g_prob = jax.nn.softmax((xt @ w_gr + b_gr).astype(jnp.float32), axis=-1)
    g_p, g_idx = lax.top_k(g_prob, 1)
    e_logits = (xt @ w_er.reshape(D, N_EXPERTS) + b_er.reshape(N_EXPERTS)).astype(jnp.float32)
    e_logits = e_logits.reshape(N, N_GROUPS, EXPERTS_PER_GROUP)
    e_sel = jnp.take_along_axis(e_logits, g_idx[:, :, None], axis=1)[:, 0]
    e_top, e_idx = lax.top_k(e_sel, TOP_K_IN_GROUP)
    e_w = jax.nn.softmax(e_top, axis=-1) * g_p
    flat = g_idx * EXPERTS_PER_GROUP + e_idx
    gate = jnp.sum(jax.nn.one_hot(flat, N_EXPERTS, dtype=jnp.float32) * e_w[..., None], axis=1)
    h = jax.nn.silu(jnp.einsum('nd,edf->nef', xt, w1)) * jnp.einsum('nd,edf->nef', xt, w3)
    h = h * gate[..., None].astype(h.dtype)
    return jnp.einsum('nef,efd->nd', h, w2).reshape(B, T, D)


def hybrid_layer(x, pos, ret_S0, conv_buf0, attend, p):
    h = jnp.einsum('btd,de->bte', x, p['w_in'])
    rq, rk, rv, rg, cq, ckv, kr, pw = split_in(h)
    ret_out, ret_S = retention_mixer(rq, rk, rv, rg, pos, ret_S0, p['ret_gn_g'], p['ret_gn_b'])
    q_nope, q_rope, latent, k_rope = mla_project(cq, ckv, kr, pos, p['mla_q_norm_g'], p['mla_kv_norm_g'], p['mla_w_uq'])
    mla_out = attend(q_nope, q_rope, latent, k_rope, p['mla_w_uk'], p['mla_w_uv'])
    conv_out, conv_buf = conv_mixer(pw, conv_buf0, p['conv_w_dw'], p['conv_b_dw'], p['conv_gn_g'], p['conv_gn_b'])
    mixed = jnp.concatenate([ret_out.astype(x.dtype), mla_out.astype(x.dtype), conv_out.astype(x.dtype)], axis=-1)
    x = layer_norm(DEEPNORM_ALPHA * x + mixed @ p['w_out'], p['ln1_g'], p['ln1_b'])
    ffn = hier_moe(x, p['moe_w_group'], p['moe_b_group'], p['moe_w_expert'], p['moe_b_expert'], p['moe_w1'], p['moe_w3'], p['moe_w2'])
    x = layer_norm(DEEPNORM_ALPHA * x + ffn, p['ln2_g'], p['ln2_b'])
    return x, latent, k_rope, ret_S, conv_buf


def setup_inputs(seed: int = 0) -> dict:
    key = jax.random.key(seed)
    ks = iter(jax.random.split(key, 40))
    f32 = jnp.float32

    def normal(shape, scale):
        return jax.random.normal(next(ks), shape, f32) * scale

    n_pages = PAST_LEN // PAGE_SIZE
    n_used = DEC_BATCH * n_pages
    n_phys = n_used + max(1, n_used // 4)
    x_prompt = normal((BATCH, SEQ, D_MODEL), 1.0)
    x_sample = normal((DEC_BATCH, DEC_SEQ, D_MODEL), 1.0)
    page_table = jax.random.permutation(next(ks), n_phys)[:n_used].reshape(DEC_BATCH, n_pages).astype(jnp.int32)
    kv_keys = jax.random.split(next(ks), DEPTH)
    cache_kv_latent = jax.vmap(lambda k: jax.random.normal(k, (n_phys, PAGE_SIZE, MLA_KV_LORA), f32))(kv_keys)
    kr_keys = jax.random.split(next(ks), DEPTH)
    cache_k_rope = jax.vmap(lambda k: jax.random.normal(k, (n_phys, PAGE_SIZE, MLA_ROPE), f32))(kr_keys)
    state_ret = normal((DEPTH, DEC_BATCH, RET_HEADS, RET_DK, RET_DV), 0.1)
    state_conv = normal((DEPTH, DEC_BATCH, CONV_WIDTH - 1, CONV_CH), 0.5)
    return {
        'x_prompt': x_prompt,
        'x_sample': x_sample,
        'cache_kv_latent': cache_kv_latent,
        'cache_k_rope': cache_k_rope,
        'state_ret': state_ret,
        'state_conv': state_conv,
        'page_table': page_table,
        'w_in': normal((DEPTH, D_MODEL, D_IN), D_MODEL ** -0.5),
        'w_out': normal((DEPTH, D_MIX, D_MODEL), DEEPNORM_BETA * D_MIX ** -0.5),
        'ret_gn_g': 1.0 + normal((DEPTH, RET_HEADS * RET_DV), 0.02),
        'ret_gn_b': normal((DEPTH, RET_HEADS * RET_DV), 0.02),
        'mla_q_norm_g': 1.0 + normal((DEPTH, MLA_Q_LORA), 0.02),
        'mla_kv_norm_g': 1.0 + normal((DEPTH, MLA_KV_LORA), 0.02),
        'mla_w_uq': normal((DEPTH, MLA_Q_LORA, MLA_HEADS, MLA_NOPE + MLA_ROPE), MLA_Q_LORA ** -0.5),
        'mla_w_uk': normal((DEPTH, MLA_KV_LORA, MLA_HEADS, MLA_NOPE), MLA_KV_LORA ** -0.5),
        'mla_w_uv': normal((DEPTH, MLA_KV_LORA, MLA_HEADS, MLA_V), MLA_KV_LORA ** -0.5),
        'conv_w_dw': normal((DEPTH, CONV_WIDTH, CONV_CH), CONV_WIDTH ** -0.5),
        'conv_b_dw': normal((DEPTH, CONV_CH), 0.02),
        'conv_gn_g': 1.0 + normal((DEPTH, CONV_CH), 0.02),
        'conv_gn_b': normal((DEPTH, CONV_CH), 0.02),
        'ln1_g': 1.0 + normal((DEPTH, D_MODEL), 0.02),
        'ln1_b': normal((DEPTH, D_MODEL), 0.02),
        'ln2_g': 1.0 + normal((DEPTH, D_MODEL), 0.02),
        'ln2_b': normal((DEPTH, D_MODEL), 0.02),
        'moe_w_group': normal((DEPTH, D_MODEL, N_GROUPS), D_MODEL ** -0.5),
        'moe_b_group': normal((DEPTH, N_GROUPS), 0.01),
        'moe_w_expert': normal((DEPTH, D_MODEL, N_GROUPS, EXPERTS_PER_GROUP), D_MODEL ** -0.5),
        'moe_b_expert': normal((DEPTH, N_GROUPS, EXPERTS_PER_GROUP), 0.01),
        'moe_w1': normal((DEPTH, N_EXPERTS, D_MODEL, D_EXPERT), D_MODEL ** -0.5),
        'moe_w3': normal((DEPTH, N_EXPERTS, D_MODEL, D_EXPERT), D_MODEL ** -0.5),
        'moe_w2': normal((DEPTH, N_EXPERTS, D_EXPERT, D_MODEL), DEEPNORM_BETA * D_EXPERT ** -0.5),
    }


def reference(x_prompt, x_sample, cache_kv_latent, cache_k_rope, state_ret, state_conv, page_table,
              w_in, w_out, ret_gn_g, ret_gn_b, mla_q_norm_g, mla_kv_norm_g, mla_w_uq, mla_w_uk, mla_w_uv,
              conv_w_dw, conv_b_dw, conv_gn_g, conv_gn_b, ln1_g, ln1_b, ln2_g, ln2_b,
              moe_w_group, moe_b_group, moe_w_expert, moe_b_expert, moe_w1, moe_w3, moe_w2):
    B, S = x_prompt.shape[0], x_prompt.shape[1]
    DB, T = x_sample.shape[0], x_sample.shape[1]
    past = page_table.shape[1] * cache_kv_latent.shape[2]
    pos_p = jnp.arange(S, dtype=jnp.int32)
    pos_s = past + jnp.arange(T, dtype=jnp.int32)
    hp, hs = x_prompt, x_sample
    p_lat, p_kr, p_ret, p_conv = [], [], [], []
    s_lat, s_kr, s_ret, s_conv = [], [], [], []
    for l in range(DEPTH):
        p = {
            'w_in': w_in[l], 'w_out': w_out[l], 'ret_gn_g': ret_gn_g[l], 'ret_gn_b': ret_gn_b[l],
            'mla_q_norm_g': mla_q_norm_g[l], 'mla_kv_norm_g': mla_kv_norm_g[l], 'mla_w_uq': mla_w_uq[l],
            'mla_w_uk': mla_w_uk[l], 'mla_w_uv': mla_w_uv[l], 'conv_w_dw': conv_w_dw[l], 'conv_b_dw': conv_b_dw[l],
            'conv_gn_g': conv_gn_g[l], 'conv_gn_b': conv_gn_b[l], 'ln1_g': ln1_g[l], 'ln1_b': ln1_b[l],
            'ln2_g': ln2_g[l], 'ln2_b': ln2_b[l], 'moe_w_group': moe_w_group[l], 'moe_b_group': moe_b_group[l],
            'moe_w_expert': moe_w_expert[l], 'moe_b_expert': moe_b_expert[l], 'moe_w1': moe_w1[l],
            'moe_w3': moe_w3[l], 'moe_w2': moe_w2[l],
        }
        hp, lat, kr, S_ret, buf = hybrid_layer(
            hp, pos_p, jnp.zeros((B, RET_HEADS, RET_DK, RET_DV), jnp.float32),
            jnp.zeros((B, CONV_WIDTH - 1, CONV_CH), hp.dtype), mla_prompt_attention, p)
        p_lat.append(lat)
        p_kr.append(kr)
        p_ret.append(S_ret.astype(state_ret.dtype))
        p_conv.append(buf.astype(state_conv.dtype))
        attend = functools.partial(mla_sample_attention, cache_kv=cache_kv_latent, cache_kr=cache_k_rope,
                                   layer=l, page_table=page_table)
        hs, lat, kr, S_ret, buf = hybrid_layer(hs, pos_s, state_ret[l], state_conv[l], attend, p)
        s_lat.append(lat)
        s_kr.append(kr)
        s_ret.append(S_ret.astype(state_ret.dtype))
        s_conv.append(buf.astype(state_conv.dtype))
    prompt_kv_latent = jnp.stack(p_lat)
    prompt_k_rope = jnp.stack(p_kr)
    prompt_state_ret = jnp.stack(p_ret)
    prompt_state_conv = jnp.stack(p_conv)
    sample_kv_latent = jnp.stack(s_lat)
    sample_k_rope = jnp.stack(s_kr)
    sample_state_ret = jnp.stack(s_ret)
    sample_state_conv = jnp.stack(s_conv)
    return (hp, hs, prompt_kv_latent, prompt_k_rope, prompt_state_ret, prompt_state_conv,
            sample_kv_latent, sample_k_rope, sample_state_ret, sample_state_conv)
```

```python
import functools

import jax
import jax.numpy as jnp
from jax import lax
from jax.experimental import pallas as pl
from jax.experimental.pallas import tpu as pltpu

F32 = jnp.float32
BF16 = jnp.bfloat16

RET_HEADS, RET_DK, RET_DV, RET_CHUNK = 6, 128, 256, 128
MLA_HEADS, MLA_Q_LORA, MLA_KV_LORA, MLA_NOPE, MLA_ROPE, MLA_V = 12, 768, 512, 128, 64, 128
MLA_SCALE = (MLA_NOPE + MLA_ROPE) ** -0.5
CONV_CH, CONV_WIDTH, CONV_GROUPS = 1024, 31, 8
N_GROUPS, EXPERTS_PER_GROUP, D_EXPERT = 4, 8, 256
N_EXPERTS = N_GROUPS * EXPERTS_PER_GROUP
ROPE_BASE = 10000.0
NORM_EPS = 1e-5
NEG_INF = -1e30
DEPTH = 2
DEEPNORM_ALPHA = (2 * DEPTH) ** 0.25

LANES = 128
MXU_DIM = 256
VMEM_LIMIT_MAX = 56 * 1024 * 1024

H_RQ, H_RK, H_RV, H_RG = 0, 768, 1536, 3072
H_CQ, H_KR, H_CKV, H_CA, H_CG = 4608, 5376, 5632, 6144, 7168
H_KR_W = 256
H_TOT = 8192
Q_HEAD_W = 256
ROUTER_W = LANES
EXPERT_TILE = 128


def _cparams(semantics, vmem_bytes):
    return pltpu.CompilerParams(dimension_semantics=semantics,
                                vmem_limit_bytes=int(min(max(vmem_bytes, 16 * 1024 * 1024), VMEM_LIMIT_MAX)))


def _row_tile(n, candidates=(640, 512, 256, 128)):
    for c in candidates:
        if n % c == 0:
            return c
    raise ValueError(f"row count {n} is not a multiple of {candidates[-1]}")


def _rope_half(x, cos, sin, half):
    n = x.shape[-1]
    if 2 * half == n:
        swapped = pltpu.roll(x, half, axis=x.ndim - 1)
    else:
        lane = lax.broadcasted_iota(jnp.int32, x.shape, x.ndim - 1)
        swapped = jnp.where(lane < half, pltpu.roll(x, n - half, axis=x.ndim - 1),
                            pltpu.roll(x, half, axis=x.ndim - 1))
    return x * cos + swapped * sin


def _layer_norm_rows(x, g, b):
    mean = jnp.mean(x, axis=-1, keepdims=True)
    xc = x - mean
    var = jnp.mean(xc * xc, axis=-1, keepdims=True)
    return xc * lax.rsqrt(var + NORM_EPS) * g + b


def _silu(x):
    return x * jax.nn.sigmoid(x)


def _dot(a, b):
    return jnp.dot(a, b, preferred_element_type=F32)


def _dot_nt(a, b):
    return lax.dot_general(a, b, (((1,), (1,)), ((), ())), preferred_element_type=F32)


def _dot_tn(a, b):
    return lax.dot_general(a, b, (((0,), (0,)), ((), ())), preferred_element_type=F32)


def _mm_kernel(a_ref, b_ref, o_ref):
    o_ref[...] = _dot(a_ref[...], b_ref[...]).astype(o_ref.dtype)


def _matmul(a, b, out_dtype, tn, name):
    m, k = a.shape
    n = b.shape[1]
    tm = _row_tile(m)
    vmem = 2 * (tm * k * 2 + k * tn * 2 + tm * tn * 4) + tm * tn * 4 + (8 << 20)
    return pl.pallas_call(
        _mm_kernel, grid=(m // tm, n // tn),
        in_specs=[pl.BlockSpec((tm, k), lambda i, j: (i, 0)), pl.BlockSpec((k, tn), lambda i, j: (0, j))],
        out_specs=pl.BlockSpec((tm, tn), lambda i, j: (i, j)),
        out_shape=jax.ShapeDtypeStruct((m, n), out_dtype),
        compiler_params=_cparams(("parallel", "parallel"), vmem), name=name)(a, b)


def _mla_prep_kernel(cq_ref, kr_ref, ckv_ref, cos_ref, sin_ref, qg_ref, kvg_ref,
                     cqn_ref, lat_ref, latb_ref, krope_ref):
    cq = cq_ref[...]
    cqn = cq * lax.rsqrt(jnp.mean(cq * cq, axis=-1, keepdims=True) + NORM_EPS) * qg_ref[...]
    cqn_ref[...] = cqn.astype(BF16)
    ckv = ckv_ref[...]
    lat = ckv * lax.rsqrt(jnp.mean(ckv * ckv, axis=-1, keepdims=True) + NORM_EPS) * kvg_ref[...]
    lat_ref[...] = lat
    latb_ref[...] = lat.astype(BF16)
    kr = kr_ref[...][:, :LANES]
    krope_ref[...] = _rope_half(kr, cos_ref[...], sin_ref[...], MLA_ROPE // 2)


def _mla_prep(h, cos64, sin64, q_norm_g, kv_norm_g):
    n = h.shape[0]
    tm = _row_tile(n)
    return pl.pallas_call(
        _mla_prep_kernel, grid=(n // tm,),
        in_specs=[pl.BlockSpec((tm, MLA_Q_LORA), lambda i: (i, H_CQ // MLA_Q_LORA)),
                  pl.BlockSpec((tm, H_KR_W), lambda i: (i, H_KR // H_KR_W)),
                  pl.BlockSpec((tm, MLA_KV_LORA), lambda i: (i, H_CKV // MLA_KV_LORA)),
                  pl.BlockSpec((tm, LANES), lambda i: (i, 0)),
                  pl.BlockSpec((tm, LANES), lambda i: (i, 0)),
                  pl.BlockSpec((1, MLA_Q_LORA), lambda i: (0, 0)),
                  pl.BlockSpec((1, MLA_KV_LORA), lambda i: (0, 0))],
        out_specs=[pl.BlockSpec((tm, MLA_Q_LORA), lambda i: (i, 0)),
                   pl.BlockSpec((tm, MLA_KV_LORA), lambda i: (i, 0)),
                   pl.BlockSpec((tm, MLA_KV_LORA), lambda i: (i, 0)),
                   pl.BlockSpec((tm, LANES), lambda i: (i, 0))],
        out_shape=[jax.ShapeDtypeStruct((n, MLA_Q_LORA), BF16),
                   jax.ShapeDtypeStruct((n, MLA_KV_LORA), F32),
                   jax.ShapeDtypeStruct((n, MLA_KV_LORA), BF16),
                   jax.ShapeDtypeStruct((n, LANES), F32)],
        compiler_params=_cparams(("parallel",), 24 << 20), name="mla_prep",
    )(h, h, h, cos64, sin64, q_norm_g, kv_norm_g)


def _qproj_kernel(a_ref, w_ref, cos_ref, sin_ref, o_ref):
    acc = _dot(a_ref[...], w_ref[...])
    o_ref[:, :MLA_NOPE] = acc[:, :MLA_NOPE].astype(BF16)
    o_ref[:, MLA_NOPE:] = _rope_half(acc[:, MLA_NOPE:], cos_ref[...], sin_ref[...], MLA_ROPE // 2).astype(BF16)


def _qproj(cqn, w_uq, cos64, sin64):
    n = cqn.shape[0]
    tm = _row_tile(n)
    return pl.pallas_call(
        _qproj_kernel, grid=(n // tm, MLA_HEADS),
        in_specs=[pl.BlockSpec((tm, MLA_Q_LORA), lambda i, h: (i, 0)),
                  pl.BlockSpec((MLA_Q_LORA, Q_HEAD_W), lambda i, h: (0, h)),
                  pl.BlockSpec((tm, LANES), lambda i, h: (i, 0)),
                  pl.BlockSpec((tm, LANES), lambda i, h: (i, 0))],
        out_specs=pl.BlockSpec((tm, Q_HEAD_W), lambda i, h: (i, h)),
        out_shape=jax.ShapeDtypeStruct((n, MLA_HEADS * Q_HEAD_W), BF16),
        compiler_params=_cparams(("parallel", "parallel"), 16 << 20), name="mla_qproj",
    )(cqn, w_uq, cos64, sin64)


def _kvup_kernel(lat_ref, kr_ref, wk_ref, wv_ref, k_ref, v_ref):
    lat = lat_ref[...]
    kn = _dot(lat, wk_ref[...])
    krb = kr_ref[...].astype(BF16)
    for hh in range(MLA_HEADS):
        k_ref[:, hh * Q_HEAD_W:hh * Q_HEAD_W + MLA_NOPE] = kn[:, hh * MLA_NOPE:(hh + 1) * MLA_NOPE].astype(BF16)
        k_ref[:, hh * Q_HEAD_W + MLA_NOPE:(hh + 1) * Q_HEAD_W] = krb
    v_ref[...] = _dot(lat, wv_ref[...]).astype(BF16)


def _kvup(latb, krope, w_uk, w_uv, n_prompt):
    tm = _row_tile(n_prompt, (512, 256, 128))
    wn = MLA_HEADS * MLA_NOPE
    return pl.pallas_call(
        _kvup_kernel, grid=(n_prompt // tm,),
        in_specs=[pl.BlockSpec((tm, MLA_KV_LORA), lambda i: (i, 0)),
                  pl.BlockSpec((tm, LANES), lambda i: (i, 0)),
                  pl.BlockSpec((MLA_KV_LORA, wn), lambda i: (0, 0)),
                  pl.BlockSpec((MLA_KV_LORA, MLA_HEADS * MLA_V), lambda i: (0, 0))],
        out_specs=[pl.BlockSpec((tm, MLA_HEADS * Q_HEAD_W), lambda i: (i, 0)),
                   pl.BlockSpec((tm, MLA_HEADS * MLA_V), lambda i: (i, 0))],
        out_shape=[jax.ShapeDtypeStruct((n_prompt, MLA_HEADS * Q_HEAD_W), BF16),
                   jax.ShapeDtypeStruct((n_prompt, MLA_HEADS * MLA_V), BF16)],
        compiler_params=_cparams(("parallel",), 32 << 20), name="mla_kvup",
    )(latb, krope, w_uk, w_uv)


def _attn_p_kernel(q_ref, k_ref, v_ref, init_ref, o_ref, *, tq, tk):
    del init_ref
    i = pl.program_id(2)
    q = q_ref[...]
    row = lax.broadcasted_iota(jnp.int32, (tq, tk), 0) + i * tq
    col = lax.broadcasted_iota(jnp.int32, (tq, tk), 1)

    def body(j, carry):
        m, l, acc = carry
        start = pl.multiple_of(j * tk, tk)
        k = k_ref[pl.ds(start, tk), :]
        v = v_ref[pl.ds(start, tk), :]
        s = _dot_nt(q, k) * MLA_SCALE
        s = jnp.where(col + j * tk <= row, s, NEG_INF)
        m_new = jnp.maximum(m, jnp.max(s, axis=-1, keepdims=True))
        corr = jnp.exp(m - m_new)
        p = jnp.exp(s - m_new)
        l = l * corr + jnp.sum(p, axis=-1, keepdims=True)
        acc = acc * corr + _dot(p.astype(BF16), v)
        return m_new, l, acc

    init = (jnp.full((tq, 1), NEG_INF, F32), jnp.zeros((tq, 1), F32), jnp.zeros((tq, MLA_V), F32))
    n_kv = (i * tq + tq + tk - 1) // tk
    _, l, acc = lax.fori_loop(0, n_kv, body, init)
    o_ref[...] = (acc / l).astype(o_ref.dtype)


def _attn_prompt(q_full, k_full, v_full, batch, seq, n_tok):
    tq = tk = 256 if seq % 256 == 0 else 128
    nq = seq // tq
    kern = functools.partial(_attn_p_kernel, tq=tq, tk=tk)
    return pl.pallas_call(
        kern, grid=(batch, MLA_HEADS, nq),
        in_specs=[pl.BlockSpec((tq, Q_HEAD_W), lambda b, h, i: (b * nq + i, h)),
                  pl.BlockSpec((seq, Q_HEAD_W), lambda b, h, i: (b, h)),
                  pl.BlockSpec((seq, MLA_V), lambda b, h, i: (b, h)),
                  pl.BlockSpec(memory_space=pl.ANY)],
        out_specs=pl.BlockSpec((tq, MLA_V), lambda b, h, i: (b * nq + i, h)),
        out_shape=jax.ShapeDtypeStruct((n_tok, MLA_HEADS * MLA_V), BF16),
        input_output_aliases={3: 0},
        compiler_params=_cparams(("parallel", "parallel", "arbitrary"), 24 << 20), name="mla_attn_prompt",
    )(q_full, k_full, v_full, jnp.zeros((n_tok, MLA_HEADS * MLA_V), BF16))


def _qlat_kernel(q_ref, wk_ref, o_ref):
    qn = q_ref[...][:, :MLA_NOPE]
    o_ref[0] = (_dot_nt(qn, wk_ref[...]) * MLA_SCALE).astype(BF16)


def _qlat(q_full, w_uk, n_prompt, n_sample):
    rb = n_prompt // n_sample
    return pl.pallas_call(
        _qlat_kernel, grid=(MLA_HEADS,),
        in_specs=[pl.BlockSpec((n_sample, Q_HEAD_W), lambda h: (rb, h)),
                  pl.BlockSpec((MLA_KV_LORA, MLA_NOPE), lambda h: (0, h))],
        out_specs=pl.BlockSpec((1, n_sample, MLA_KV_LORA), lambda h: (h, 0, 0)),
        out_shape=jax.ShapeDtypeStruct((MLA_HEADS, n_sample, MLA_KV_LORA), BF16),
        compiler_params=_cparams(("parallel",), 16 << 20), name="mla_qlat",
    )(q_full, w_uk)


def _attn_s_kernel(pt_ref, qlat_ref, qr_ref, lat_ref, kr_ref, *rest, pps, n_steps):
    kv_refs = rest[:pps]
    kr_refs = rest[pps:2 * pps]
    o_ref, m_sc, l_sc, acc_sc = rest[2 * pps:]
    step = pl.program_id(1)

    @pl.when(step == 0)
    def _():
        m_sc[...] = jnp.full(m_sc.shape, NEG_INF, F32)
        l_sc[...] = jnp.zeros(l_sc.shape, F32)
        acc_sc[...] = jnp.zeros(acc_sc.shape, F32)

    qlat = qlat_ref[0]
    qr = qr_ref[0]
    kvs, ss = [], []
    for i in range(pps):
        kvb = kv_refs[i][...].astype(BF16)
        krb = kr_refs[i][...].astype(BF16)
        kvs.append(kvb)
        ss.append(_dot_nt(qlat, kvb) + _dot_nt(qr, krb))
    m_prev = m_sc[...]
    m_new = m_prev
    for s in ss:
        m_new = jnp.maximum(m_new, jnp.max(s, axis=-1, keepdims=True))
    corr = jnp.exp(m_prev - m_new)
    l = l_sc[...] * corr
    acc = acc_sc[...] * corr
    for s, kvb in zip(ss, kvs):
        p = jnp.exp(s - m_new)
        l = l + jnp.sum(p, axis=-1, keepdims=True)
        acc = acc + _dot(p.astype(BF16), kvb)
    m_sc[...] = m_new
    l_sc[...] = l
    acc_sc[...] = acc

    @pl.when(step == n_steps - 1)
    def _():
        lat = lat_ref[0]
        kr = kr_ref[0][:, :MLA_ROPE]
        s = (jnp.sum(qlat.astype(F32) * lat, axis=-1, keepdims=True)
             + jnp.sum(qr.astype(F32) * kr, axis=-1, keepdims=True))
        m_fin = jnp.maximum(m_new, s)
        c = jnp.exp(m_new - m_fin)
        p = jnp.exp(s - m_fin)
        o_ref[0] = (acc * c + p * lat) / (l * c + p)


def _attn_sample(page_table, qlat, qr, lat_s, kr_s, cache_kv, cache_kr, layer):
    db, n_pages = page_table.shape
    page = cache_kv.shape[2]
    pps = 16
    while n_pages % pps:
        pps //= 2
    n_steps = n_pages // pps
    hp = qlat.shape[1]
    kern = functools.partial(_attn_s_kernel, pps=pps, n_steps=n_steps)

    def page_map(i):
        return lambda b, s, pt: (layer, pt[b * n_pages + s * pps + i], 0, 0)

    in_specs = [pl.BlockSpec((1, hp, MLA_KV_LORA), lambda b, s, pt: (b, 0, 0)),
                pl.BlockSpec((1, hp, MLA_ROPE), lambda b, s, pt: (b, 0, 0)),
                pl.BlockSpec((1, 1, MLA_KV_LORA), lambda b, s, pt: (b, 0, 0)),
                pl.BlockSpec((1, 1, LANES), lambda b, s, pt: (b, 0, 0))]
    in_specs += [pl.BlockSpec((None, None, page, MLA_KV_LORA), page_map(i)) for i in range(pps)]
    in_specs += [pl.BlockSpec((None, None, page, MLA_ROPE), page_map(i)) for i in range(pps)]
    vmem = 2 * pps * page * (MLA_KV_LORA + LANES) * 4 + pps * page * MLA_KV_LORA * 2 + (8 << 20)
    return pl.pallas_call(
        kern,
        grid_spec=pltpu.PrefetchScalarGridSpec(
            num_scalar_prefetch=1, grid=(db, n_steps), in_specs=in_specs,
            out_specs=pl.BlockSpec((1, hp, MLA_KV_LORA), lambda b, s, pt: (b, 0, 0)),
            scratch_shapes=[pltpu.VMEM((hp, 1), F32), pltpu.VMEM((hp, 1), F32),
                            pltpu.VMEM((hp, MLA_KV_LORA), F32)]),
        out_shape=jax.ShapeDtypeStruct((db, hp, MLA_KV_LORA), F32),
        compiler_params=_cparams(("parallel", "arbitrary"), vmem), name="mla_attn_sample",
    )(page_table.reshape(-1), qlat, qr, lat_s, kr_s, *([cache_kv] * pps), *([cache_kr] * pps))


def _ctxproj_kernel(ctx_ref, wv_ref, mixed_ref, o_ref):
    del mixed_ref
    o_ref[...] = _dot(ctx_ref[0].astype(BF16), wv_ref[...]).astype(o_ref.dtype)


def _ctxproj(ctx_t, w_uv, mixed_mla, n_prompt):
    n_sample = ctx_t.shape[1]
    rb = n_prompt // n_sample
    return pl.pallas_call(
        _ctxproj_kernel, grid=(MLA_HEADS,),
        in_specs=[pl.BlockSpec((1, n_sample, MLA_KV_LORA), lambda h: (h, 0, 0)),
                  pl.BlockSpec((MLA_KV_LORA, MLA_V), lambda h: (0, h)),
                  pl.BlockSpec(memory_space=pl.ANY)],
        out_specs=pl.BlockSpec((n_sample, MLA_V), lambda h: (rb, h)),
        out_shape=jax.ShapeDtypeStruct(mixed_mla.shape, mixed_mla.dtype),
        input_output_aliases={2: 0},
        compiler_params=_cparams(("parallel",), 16 << 20), name="mla_ctxproj",
    )(ctx_t, w_uv, mixed_mla)


def _ret_p_kernel(rq_ref, rk_ref, rv_ref, rg_ref, cos_ref, sin_ref, decay_ref, cross_ref, kdec_ref, gc_ref,
                  gng_ref, gnb_ref, init_ref, o_ref, s_out_ref, s_sc, *, n_chunks):
    del init_ref
    c = pl.program_id(1)

    @pl.when(c == 0)
    def _():
        s_sc[...] = jnp.zeros(s_sc.shape, F32)

    cos, sin = cos_ref[...], sin_ref[...]
    for hh in range(RET_HEADS):
        q = _rope_half(rq_ref[:, hh * RET_DK:(hh + 1) * RET_DK], cos, sin, RET_DK // 2)
        k = _rope_half(rk_ref[:, hh * RET_DK:(hh + 1) * RET_DK], cos, sin, RET_DK // 2) * (RET_DK ** -0.5)
        v = rv_ref[:, hh * RET_DV:(hh + 1) * RET_DV].astype(BF16)
        qb = q.astype(BF16)
        s_old = s_sc[hh]
        scores = _dot_nt(qb, k.astype(BF16)) * decay_ref[hh]
        o = _dot(scores.astype(BF16), v) + _dot(qb, s_old.astype(BF16)) * cross_ref[hh]
        kd = (k * kdec_ref[hh]).astype(BF16)
        s_sc[hh] = gc_ref[hh] * s_old + _dot_tn(kd, v)
        sl = slice(hh * RET_DV, (hh + 1) * RET_DV)
        on = _layer_norm_rows(o, gng_ref[:, sl], gnb_ref[:, sl])
        o_ref[:, sl] = (_silu(rg_ref[:, sl]) * on).astype(o_ref.dtype)

    @pl.when(c == n_chunks - 1)
    def _():
        s_out_ref[0] = s_sc[...]


def _ret_consts(chunk):
    hidx = jnp.arange(RET_HEADS, dtype=F32)
    log_g = jnp.log1p(-jnp.exp2(-5.0 - hidx))
    idx = jnp.arange(chunk, dtype=F32)
    diff = idx[:, None] - idx[None, :]
    decay = jnp.where(diff >= 0, jnp.exp(log_g[:, None, None] * jnp.maximum(diff, 0.0)[None]), 0.0)
    cross = jnp.exp((idx[:, None] + 1.0) * log_g[None, :]).T
    kdec = jnp.exp((chunk - 1.0 - idx)[:, None] * log_g[None, :]).T
    gc = jnp.exp(chunk * log_g)
    return log_g, decay, cross, kdec, gc


def _ret_prompt(h, cos128, sin128, gn_g, gn_b, batch, seq, n_tok):
    chunk = RET_CHUNK if seq % RET_CHUNK == 0 else seq
    n_chunks = seq // chunk
    _, decay, cross, kdec, gc = _ret_consts(chunk)
    cross_b = jnp.broadcast_to(cross[:, :, None], (RET_HEADS, chunk, RET_DV))
    kdec_b = jnp.broadcast_to(kdec[:, :, None], (RET_HEADS, chunk, RET_DK))
    gc_b = jnp.broadcast_to(gc[:, None, None], (RET_HEADS, RET_DK, RET_DV))
    qw, vw = RET_HEADS * RET_DK, RET_HEADS * RET_DV
    kern = functools.partial(_ret_p_kernel, n_chunks=n_chunks)
    rows = lambda b, c: b * n_chunks + c
    const3 = lambda b, c: (0, 0, 0)
    return pl.pallas_call(
        kern, grid=(batch, n_chunks),
        in_specs=[pl.BlockSpec((chunk, qw), lambda b, c: (rows(b, c), H_RQ // qw)),
                  pl.BlockSpec((chunk, qw), lambda b, c: (rows(b, c), H_RK // qw)),
                  pl.BlockSpec((chunk, vw), lambda b, c: (rows(b, c), H_RV // vw)),
                  pl.BlockSpec((chunk, vw), lambda b, c: (rows(b, c), H_RG // vw)),
                  pl.BlockSpec((chunk, RET_DK), lambda b, c: (c, 0)),
                  pl.BlockSpec((chunk, RET_DK), lambda b, c: (c, 0)),
                  pl.BlockSpec((RET_HEADS, chunk, chunk), const3),
                  pl.BlockSpec((RET_HEADS, chunk, RET_DV), const3),
                  pl.BlockSpec((RET_HEADS, chunk, RET_DK), const3),
                  pl.BlockSpec((RET_HEADS, RET_DK, RET_DV), const3),
                  pl.BlockSpec((1, vw), lambda b, c: (0, 0)),
                  pl.BlockSpec((1, vw), lambda b, c: (0, 0)),
                  pl.BlockSpec(memory_space=pl.ANY)],
        out_specs=[pl.BlockSpec((chunk, vw), lambda b, c: (rows(b, c), 0)),
                   pl.BlockSpec((1, RET_HEADS, RET_DK, RET_DV), lambda b, c: (b, 0, 0, 0))],
        out_shape=[jax.ShapeDtypeStruct((n_tok, vw), BF16),
                   jax.ShapeDtypeStruct((batch, RET_HEADS, RET_DK, RET_DV), F32)],
        input_output_aliases={12: 0},
        scratch_shapes=[pltpu.VMEM((RET_HEADS, RET_DK, RET_DV), F32)],
        compiler_params=_cparams(("parallel", "arbitrary"), 32 << 20), name="ret_prompt",
    )(h, h, h, h, cos128, sin128, decay, cross_b, kdec_b, gc_b, gn_g, gn_b, jnp.zeros((n_tok, vw), BF16))


def _ret_s_kernel(rq_ref, rk_ref, rv_ref, rg_ref, cos_ref, sin_ref, g_ref, gng_ref, gnb_ref, s_ref, mixed_ref,
                  o_ref, s_out_ref, *, bb):
    del mixed_ref
    cos, sin = cos_ref[...], sin_ref[...]
    q = _rope_half(rq_ref[...], cos, sin, RET_DK // 2)
    k = _rope_half(rk_ref[...], cos, sin, RET_DK // 2) * (RET_DK ** -0.5)
    qk = jnp.sum(q * k, axis=-1, keepdims=True)
    qt, kt = q.T, k.T
    v = rv_ref[...]
    g = g_ref[0]
    rows = []
    for b in range(bb):
        s_old = s_ref[b, 0]
        vb = v[b:b + 1, :]
        qs = jnp.sum(qt[:, b:b + 1] * s_old, axis=0, keepdims=True)
        rows.append(qk[b:b + 1, :] * vb + qs * g)
        s_out_ref[b, 0] = g * s_old + kt[:, b:b + 1] * vb
    o = jnp.concatenate(rows, axis=0)
    on = _layer_norm_rows(o, gng_ref[...], gnb_ref[...])
    o_ref[...] = (_silu(rg_ref[...]) * on).astype(o_ref.dtype)


def _ret_sample(h, cos_row, sin_row, gn_g, gn_b, state, mixed_ret, n_prompt):
    db = state.shape[0]
    bb = 16
    log_g = _ret_consts(1)[0]
    g_b = jnp.broadcast_to(jnp.exp(log_g)[:, None, None], (RET_HEADS, 1, RET_DV))
    rb = n_prompt // bb
    kern = functools.partial(_ret_s_kernel, bb=bb)
    return pl.pallas_call(
        kern, grid=(db // bb, RET_HEADS),
        in_specs=[pl.BlockSpec((bb, RET_DK), lambda i, hh: (rb + i, H_RQ // RET_DK + hh)),
                  pl.BlockSpec((bb, RET_DK), lambda i, hh: (rb + i, H_RK // RET_DK + hh)),
                  pl.BlockSpec((bb, RET_DV), lambda i, hh: (rb + i, H_RV // RET_DV + hh)),
                  pl.BlockSpec((bb, RET_DV), lambda i, hh: (rb + i, H_RG // RET_DV + hh)),
                  pl.BlockSpec((1, RET_DK), lambda i, hh: (0, 0)),
                  pl.BlockSpec((1, RET_DK), lambda i, hh: (0, 0)),
                  pl.BlockSpec((1, 1, RET_DV), lambda i, hh: (hh, 0, 0)),
                  pl.BlockSpec((1, RET_DV), lambda i, hh: (0, hh)),
                  pl.BlockSpec((1, RET_DV), lambda i, hh: (0, hh)),
                  pl.BlockSpec((bb, 1, RET_DK, RET_DV), lambda i, hh: (i, hh, 0, 0)),
                  pl.BlockSpec(memory_space=pl.ANY)],
        out_specs=[pl.BlockSpec((bb, RET_DV), lambda i, hh: (rb + i, hh)),
                   pl.BlockSpec((bb, 1, RET_DK, RET_DV), lambda i, hh: (i, hh, 0, 0))],
        out_shape=[jax.ShapeDtypeStruct(mixed_ret.shape, mixed_ret.dtype),
                   jax.ShapeDtypeStruct(state.shape, F32)],
        input_output_aliases={10: 0},
        compiler_params=_cparams(("parallel", "parallel"), 4 * bb * RET_DK * RET_DV * 4 + (8 << 20)),
        name="ret_sample",
    )(h, h, h, h, cos_row, sin_row, g_b, gn_g, gn_b, state, mixed_ret)


HIST = 32


def _group_norm_silu(y, g, b):
    outs = []
    gsz = CONV_CH // CONV_GROUPS
    for gi in range(CONV_GROUPS):
        sl = slice(gi * gsz, (gi + 1) * gsz)
        outs.append(_silu(_layer_norm_rows(y[:, sl], g[:, sl], b[:, sl])))
    return outs


def _conv_p_kernel(a_ref, g_ref, w_ref, bdw_ref, gng_ref, gnb_ref, init_ref, o_ref, buf_ref, u_sc, *, tt, n_t, rc):
    del init_ref
    t = pl.program_id(1)

    @pl.when(t == 0)
    def _():
        u_sc[0:HIST, :] = jnp.zeros((HIST, CONV_CH), F32)

    @pl.when(t > 0)
    def _():
        u_sc[0:HIST, :] = u_sc[tt:tt + HIST, :]

    u_sc[HIST:HIST + tt, :] = a_ref[...] * jax.nn.sigmoid(g_ref[...])
    off = HIST - (CONV_WIDTH - 1)
    gsz = CONV_CH // CONV_GROUPS
    for r0 in range(0, tt, rc):
        for gi in range(CONV_GROUPS):
            sl = slice(gi * gsz, (gi + 1) * gsz)
            acc = jnp.zeros((rc, gsz), F32)
            for w in range(CONV_WIDTH):
                acc = acc + u_sc[r0 + off + w:r0 + off + w + rc, sl] * w_ref[w:w + 1, sl]
            y = acc + bdw_ref[:, sl]
            o_ref[r0:r0 + rc, sl] = _silu(_layer_norm_rows(y, gng_ref[:, sl], gnb_ref[:, sl])).astype(o_ref.dtype)

    @pl.when(t == n_t - 1)
    def _():
        buf_ref[0] = u_sc[HIST + tt - (CONV_WIDTH - 1):HIST + tt, :]


def _conv_prompt(h, w_dw, b_dw, gn_g, gn_b, batch, seq, n_tok):
    tt = 256 if seq % 256 == 0 else 128
    n_t = seq // tt
    kern = functools.partial(_conv_p_kernel, tt=tt, n_t=n_t, rc=64)
    rows = lambda b, t: b * n_t + t
    vec = lambda b, t: (0, 0)
    return pl.pallas_call(
        kern, grid=(batch, n_t),
        in_specs=[pl.BlockSpec((tt, CONV_CH), lambda b, t: (rows(b, t), H_CA // CONV_CH)),
                  pl.BlockSpec((tt, CONV_CH), lambda b, t: (rows(b, t), H_CG // CONV_CH)),
                  pl.BlockSpec((CONV_WIDTH, CONV_CH), vec),
                  pl.BlockSpec((1, CONV_CH), vec), pl.BlockSpec((1, CONV_CH), vec), pl.BlockSpec((1, CONV_CH), vec),
                  pl.BlockSpec(memory_space=pl.ANY)],
        out_specs=[pl.BlockSpec((tt, CONV_CH), lambda b, t: (rows(b, t), 0)),
                   pl.BlockSpec((1, CONV_WIDTH - 1, CONV_CH), lambda b, t: (b, 0, 0))],
        out_shape=[jax.ShapeDtypeStruct((n_tok, CONV_CH), BF16),
                   jax.ShapeDtypeStruct((batch, CONV_WIDTH - 1, CONV_CH), F32)],
        input_output_aliases={6: 0},
        scratch_shapes=[pltpu.VMEM((HIST + tt, CONV_CH), F32)],
        compiler_params=_cparams(("parallel", "arbitrary"), 24 << 20), name="conv_prompt",
    )(h, h, w_dw, b_dw, gn_g, gn_b, jnp.zeros((n_tok, CONV_CH), BF16))


def _conv_s_kernel(a_ref, g_ref, w_ref, bdw_ref, gng_ref, gnb_ref, buf_ref, mixed_ref, o_ref, nbuf_ref):
    del mixed_ref
    hist = CONV_WIDTH - 1
    u = a_ref[...] * jax.nn.sigmoid(g_ref[...])
    w_hist = w_ref[0:hist, :]
    rows = []
    for b in range(u.shape[0]):
        rows.append(jnp.sum(buf_ref[b] * w_hist, axis=0, keepdims=True))
        nbuf_ref[b, 0:hist - 1, :] = buf_ref[b, 1:hist, :]
        nbuf_ref[b, hist - 1:hist, :] = u[b:b + 1, :]
    y = jnp.concatenate(rows, axis=0) + u * w_ref[hist:hist + 1, :] + bdw_ref[...]
    gsz = CONV_CH // CONV_GROUPS
    for gi in range(CONV_GROUPS):
        sl = slice(gi * gsz, (gi + 1) * gsz)
        o_ref[:, sl] = _silu(_layer_norm_rows(y[:, sl], gng_ref[:, sl], gnb_ref[:, sl])).astype(o_ref.dtype)


def _conv_sample(h, w_dw, b_dw, gn_g, gn_b, state, mixed_conv, n_prompt):
    db = state.shape[0]
    bb = 16
    rb = n_prompt // bb
    hist = CONV_WIDTH - 1
    vec = lambda i: (0, 0)
    return pl.pallas_call(
        _conv_s_kernel, grid=(db // bb,),
        in_specs=[pl.BlockSpec((bb, CONV_CH), lambda i: (rb + i, H_CA // CONV_CH)),
                  pl.BlockSpec((bb, CONV_CH), lambda i: (rb + i, H_CG // CONV_CH)),
                  pl.BlockSpec((CONV_WIDTH, CONV_CH), vec),
                  pl.BlockSpec((1, CONV_CH), vec), pl.BlockSpec((1, CONV_CH), vec), pl.BlockSpec((1, CONV_CH), vec),
                  pl.BlockSpec((bb, hist, CONV_CH), lambda i: (i, 0, 0)),
                  pl.BlockSpec(memory_space=pl.ANY)],
        out_specs=[pl.BlockSpec((bb, CONV_CH), lambda i: (rb + i, 0)),
                   pl.BlockSpec((bb, hist, CONV_CH), lambda i: (i, 0, 0))],
        out_shape=[jax.ShapeDtypeStruct(mixed_conv.shape, mixed_conv.dtype),
                   jax.ShapeDtypeStruct(state.shape, F32)],
        input_output_aliases={7: 0},
        compiler_params=_cparams(("parallel",), 24 << 20), name="conv_sample",
    )(h, h, w_dw, b_dw, gn_g, gn_b, state, mixed_conv)


def _outproj_kernel(a1_ref, a2_ref, a3_ref, w1_ref, w2_ref, w3_ref, x_ref, o_ref):
    z = _dot(a1_ref[...], w1_ref[...]) + _dot(a2_ref[...], w2_ref[...]) + _dot(a3_ref[...], w3_ref[...])
    o_ref[...] = DEEPNORM_ALPHA * x_ref[...] + z


def _outproj(m_ret, m_mla, m_conv, w_out, x):
    n, d = x.shape
    tm = _row_tile(n)
    tn = 1024
    k1, k2, k3 = m_ret.shape[1], m_mla.shape[1], m_conv.shape[1]
    assert k1 == k2 and (k1 + k2) % k3 == 0
    vmem = 2 * (tm * (k1 + k2 + k3) * 2 + (k1 + k2 + k3) * tn * 2 + 2 * tm * tn * 4) + (4 << 20)
    return pl.pallas_call(
        _outproj_kernel, grid=(n // tm, d // tn),
        in_specs=[pl.BlockSpec((tm, k1), lambda i, j: (i, 0)),
                  pl.BlockSpec((tm, k2), lambda i, j: (i, 0)),
                  pl.BlockSpec((tm, k3), lambda i, j: (i, 0)),
                  pl.BlockSpec((k1, tn), lambda i, j: (0, j)),
                  pl.BlockSpec((k2, tn), lambda i, j: (1, j)),
                  pl.BlockSpec((k3, tn), lambda i, j: ((k1 + k2) // k3, j)),
                  pl.BlockSpec((tm, tn), lambda i, j: (i, j))],
        out_specs=pl.BlockSpec((tm, tn), lambda i, j: (i, j)),
        out_shape=jax.ShapeDtypeStruct((n, d), F32),
        compiler_params=_cparams(("parallel", "parallel"), vmem), name="outproj",
    )(m_ret, m_mla, m_conv, w_out, w_out, w_out, x)


def _ln_router_kernel(z_ref, g_ref, b_ref, wr_ref, br_ref, x_ref, r_ref, cnt_ref, run_sc, *, n_t):
    t = pl.program_id(0)

    @pl.when(t == 0)
    def _():
        run_sc[...] = jnp.zeros(run_sc.shape, F32)

    x = _layer_norm_rows(z_ref[...], g_ref[...], b_ref[...])
    x_ref[...] = x
    tm = x.shape[0]
    logits = jnp.dot(x, wr_ref[...], preferred_element_type=F32, precision=lax.Precision.HIGHEST) + br_ref[...]
    lane = lax.broadcasted_iota(jnp.int32, (tm, ROUTER_W), 1).astype(F32)
    gl = jnp.where(lane < N_GROUPS, logits, -jnp.inf)
    gmax = jnp.max(gl, axis=-1, keepdims=True)
    g_p = 1.0 / jnp.sum(jnp.exp(gl - gmax), axis=-1, keepdims=True)
    g_idx = jnp.min(jnp.where(gl == gmax, lane, float(ROUTER_W)), axis=-1, keepdims=True)
    lo = N_GROUPS + g_idx * EXPERTS_PER_GROUP
    el = jnp.where(lane >= lo, jnp.where(lane < lo + EXPERTS_PER_GROUP, logits, -jnp.inf), -jnp.inf)
    e1 = jnp.max(el, axis=-1, keepdims=True)
    i1 = jnp.min(jnp.where(el == e1, lane, float(ROUTER_W)), axis=-1, keepdims=True)
    el2 = jnp.where(lane == i1, -jnp.inf, el)
    e2 = jnp.max(el2, axis=-1, keepdims=True)
    i2 = jnp.min(jnp.where(el2 == e2, lane, float(ROUTER_W)), axis=-1, keepdims=True)
    ex = jnp.exp(e2 - e1)
    w1 = g_p / (1.0 + ex)
    w2 = g_p * ex / (1.0 + ex)
    onehot = jnp.where(lane == i1, 1.0, jnp.where(lane == i2, 1.0, 0.0))
    tri = jnp.where(lax.broadcasted_iota(jnp.int32, (tm, tm), 1) < lax.broadcasted_iota(jnp.int32, (tm, tm), 0),
                    1.0, 0.0)
    before = _dot(tri.astype(BF16), onehot.astype(BF16)) + run_sc[...]
    rank1 = jnp.sum(jnp.where(lane == i1, before, 0.0), axis=-1, keepdims=True)
    rank2 = jnp.sum(jnp.where(lane == i2, before, 0.0), axis=-1, keepdims=True)
    run_sc[...] = run_sc[...] + jnp.sum(onehot, axis=0, keepdims=True)
    out = jnp.where(lane == 0, i1 - N_GROUPS, 0.0)
    out = jnp.where(lane == 1, i2 - N_GROUPS, out)
    out = jnp.where(lane == 2, w1, out)
    out = jnp.where(lane == 3, w2, out)
    out = jnp.where(lane == 4, rank1, out)
    out = jnp.where(lane == 5, rank2, out)
    r_ref[...] = out

    @pl.when(t == n_t - 1)
    def _():
        cnt_ref[...] = jnp.broadcast_to(run_sc[...], cnt_ref.shape)


def _ln_router(z, g, b, w_r, b_r):
    n, d = z.shape
    tm = _row_tile(n, (256, 128))
    n_t = n // tm
    kern = functools.partial(_ln_router_kernel, n_t=n_t)
    vec = lambda t: (0, 0)
    return pl.pallas_call(
        kern, grid=(n_t,),
        in_specs=[pl.BlockSpec((tm, d), lambda t: (t, 0)),
                  pl.BlockSpec((1, d), vec), pl.BlockSpec((1, d), vec),
                  pl.BlockSpec((d, ROUTER_W), vec), pl.BlockSpec((1, ROUTER_W), vec)],
        out_specs=[pl.BlockSpec((tm, d), lambda t: (t, 0)),
                   pl.BlockSpec((tm, ROUTER_W), lambda t: (t, 0)),
                   pl.BlockSpec((8, ROUTER_W), vec)],
        out_shape=[jax.ShapeDtypeStruct((n, d), F32),
                   jax.ShapeDtypeStruct((n, ROUTER_W), F32),
                   jax.ShapeDtypeStruct((8, ROUTER_W), F32)],
        scratch_shapes=[pltpu.VMEM((1, ROUTER_W), F32)],
        compiler_params=_cparams(("arbitrary",), 32 << 20), name="ln1_router",
    )(z, g, b, w_r, b_r)


def _scatter_kernel(dest_ref, x_ref, init_ref, o_ref, sem, *, tm):
    del init_ref
    t = pl.program_id(0)

    def copies(r):
        row = t * tm + r
        src = x_ref.at[pl.ds(row, 1)]
        return (pltpu.make_async_copy(src, o_ref.at[pl.ds(dest_ref[2 * row], 1)], sem),
                pltpu.make_async_copy(src, o_ref.at[pl.ds(dest_ref[2 * row + 1], 1)], sem))

    def start(r, c):
        for cp in copies(r):
            cp.start()
        return c

    def wait(r, c):
        for cp in copies(r):
            cp.wait()
        return c

    lax.fori_loop(0, tm, start, 0)
    lax.fori_loop(0, tm, wait, 0)


def _scatter_rows(x, dest_flat, n_rows_out):
    n, d = x.shape
    tm = 128
    kern = functools.partial(_scatter_kernel, tm=tm)
    init = jnp.zeros((n_rows_out, d), x.dtype)
    return pl.pallas_call(
        kern,
        grid_spec=pltpu.PrefetchScalarGridSpec(
            num_scalar_prefetch=1, grid=(n // tm,),
            in_specs=[pl.BlockSpec(memory_space=pl.ANY), pl.BlockSpec(memory_space=pl.ANY)],
            out_specs=pl.BlockSpec(memory_space=pl.ANY),
            scratch_shapes=[pltpu.SemaphoreType.DMA(())]),
        out_shape=jax.ShapeDtypeStruct((n_rows_out, d), x.dtype),
        input_output_aliases={2: 0},
        compiler_params=_cparams(("arbitrary",), 16 << 20), name="moe_scatter",
    )(dest_flat, x, init)


def _expert_kernel(te_ref, na_ref, x_ref, w1_ref, w3_ref, w2_ref, o_ref, w1_sc, w3_sc, w2_sc):
    t = pl.program_id(0)
    prev = te_ref[jnp.maximum(t - 1, 0)]
    fresh = (t == 0) | (te_ref[t] != prev)

    @pl.when(fresh & (t < na_ref[0]))
    def _():
        w1_sc[...] = w1_ref[0].astype(BF16)
        w3_sc[...] = w3_ref[0].astype(BF16)
        w2_sc[...] = w2_ref[0].astype(BF16)

    @pl.when(t < na_ref[0])
    def _():
        xb = x_ref[...].astype(BF16)
        hmid = _silu(_dot(xb, w1_sc[...])) * _dot(xb, w3_sc[...])
        o_ref[...] = _dot(hmid.astype(BF16), w2_sc[...])

    @pl.when(t >= na_ref[0])
    def _():
        o_ref[...] = jnp.zeros(o_ref.shape, o_ref.dtype)


def _experts(x_sorted, tile_expert, n_active, w1, w3, w2):
    p, d = x_sorted.shape
    n_tiles = p // EXPERT_TILE
    f = w1.shape[2]

    def row_map(t, te, na):
        return (jnp.minimum(t, na[0] - 1), 0)

    def w_map(t, te, na):
        return (te[jnp.minimum(t, na[0] - 1)], 0, 0)

    vmem = 2 * (3 * d * f * 4 + 2 * EXPERT_TILE * d * 4) + 3 * d * f * 2 + (6 << 20)
    return pl.pallas_call(
        _expert_kernel,
        grid_spec=pltpu.PrefetchScalarGridSpec(
            num_scalar_prefetch=2, grid=(n_tiles,),
            in_specs=[pl.BlockSpec((EXPERT_TILE, d), row_map),
                      pl.BlockSpec((1, d, f), w_map), pl.BlockSpec((1, d, f), w_map),
                      pl.BlockSpec((1, f, d), w_map)],
            out_specs=pl.BlockSpec((EXPERT_TILE, d), lambda t, te, na: (t, 0)),
            scratch_shapes=[pltpu.VMEM((d, f), BF16), pltpu.VMEM((d, f), BF16), pltpu.VMEM((f, d), BF16)]),
        out_shape=jax.ShapeDtypeStruct((p, d), F32),
        compiler_params=_cparams(("arbitrary",), vmem), name="moe_experts",
    )(tile_expert, n_active, x_sorted, w1, w3, w2)


def _combine_kernel(dest_ref, y_ref, x_ref, r_ref, g_ref, b_ref, o_ref, ob_ref, buf0, buf1, sem, *, tm):
    t = pl.program_id(0)

    def copies(r):
        row = t * tm + r
        return (pltpu.make_async_copy(y_ref.at[pl.ds(dest_ref[2 * row], 1)], buf0.at[pl.ds(r, 1)], sem),
                pltpu.make_async_copy(y_ref.at[pl.ds(dest_ref[2 * row + 1], 1)], buf1.at[pl.ds(r, 1)], sem))

    def start(r, c):
        for cp in copies(r):
            cp.start()
        return c

    def wait(r, c):
        for cp in copies(r):
            cp.wait()
        return c

    lax.fori_loop(0, tm, start, 0)
    lax.fori_loop(0, tm, wait, 0)
    r = r_ref[...]
    ffn = r[:, 2:3] * buf0[...] + r[:, 3:4] * buf1[...]
    out = _layer_norm_rows(DEEPNORM_ALPHA * x_ref[...] + ffn, g_ref[...], b_ref[...])
    o_ref[...] = out
    ob_ref[...] = out.astype(BF16)


def _combine_ln(y_sorted, dest_flat, x1, route, g, b):
    n, d = x1.shape
    tm = 128
    kern = functools.partial(_combine_kernel, tm=tm)
    vec = lambda t, dest: (0, 0)
    return pl.pallas_call(
        kern,
        grid_spec=pltpu.PrefetchScalarGridSpec(
            num_scalar_prefetch=1, grid=(n // tm,),
            in_specs=[pl.BlockSpec(memory_space=pl.ANY),
                      pl.BlockSpec((tm, d), lambda t, dest: (t, 0)),
                      pl.BlockSpec((tm, ROUTER_W), lambda t, dest: (t, 0)),
                      pl.BlockSpec((1, d), vec), pl.BlockSpec((1, d), vec)],
            out_specs=[pl.BlockSpec((tm, d), lambda t, dest: (t, 0)),
                       pl.BlockSpec((tm, d), lambda t, dest: (t, 0))],
            scratch_shapes=[pltpu.VMEM((tm, d), F32), pltpu.VMEM((tm, d), F32), pltpu.SemaphoreType.DMA(())]),
        out_shape=[jax.ShapeDtypeStruct((n, d), F32), jax.ShapeDtypeStruct((n, d), BF16)],
        compiler_params=_cparams(("arbitrary",), 32 << 20), name="moe_combine_ln2",
    )(dest_flat, y_sorted, x1, route, g, b)


def _route_plan(route, counts, n_tiles):
    ids = route[:, 0:2].astype(jnp.int32)
    rank = route[:, 4:6].astype(jnp.int32)
    cnt = counts[0, N_GROUPS:N_GROUPS + N_EXPERTS].astype(jnp.int32)
    tiles = (cnt + EXPERT_TILE - 1) // EXPERT_TILE
    tile_end = jnp.cumsum(tiles)
    tile_start = tile_end - tiles
    dest = (tile_start * EXPERT_TILE)[ids] + rank
    n_active = tile_end[-1:]
    tile_expert = jnp.minimum(jnp.searchsorted(tile_end, jnp.arange(n_tiles, dtype=jnp.int32), side="right"),
                              N_EXPERTS - 1).astype(jnp.int32)
    return dest.reshape(-1), tile_expert, n_active.astype(jnp.int32)


def _rope_tables(pos, d):
    inv = ROPE_BASE ** (-jnp.arange(0, d, 2, dtype=F32) / d)
    ang = pos.astype(F32)[:, None] * inv[None, :]
    c, s = jnp.cos(ang), jnp.sin(ang)
    cos = jnp.concatenate([c, c], axis=-1)
    sin = jnp.concatenate([-s, s], axis=-1)
    if d < LANES:
        cos = jnp.pad(cos, ((0, 0), (0, LANES - d)))
        sin = jnp.pad(sin, ((0, 0), (0, LANES - d)))
    return cos, sin


def _prep_weights(l, w_in, w_out, mla_w_uq, mla_w_uk, mla_w_uv, moe_w_group, moe_b_group, moe_w_expert,
                  moe_b_expert):
    d = w_in.shape[1]
    offs = [0]
    for s in (RET_HEADS * RET_DK, RET_HEADS * RET_DK, RET_HEADS * RET_DV, RET_HEADS * RET_DV, MLA_Q_LORA,
              MLA_KV_LORA, MLA_ROPE, 2 * CONV_CH):
        offs.append(offs[-1] + s)
    wi = w_in[l]
    seg = lambda i: wi[:, offs[i]:offs[i + 1]]
    w_in_p = jnp.concatenate(
        [seg(0), seg(1), seg(2), seg(3), seg(4), seg(6), jnp.zeros((d, H_KR_W - MLA_ROPE), F32), seg(5), seg(7)],
        axis=1).astype(BF16)
    uq = mla_w_uq[l]
    uq_p = jnp.pad(uq, ((0, 0), (0, 0), (0, Q_HEAD_W - MLA_NOPE - MLA_ROPE))).reshape(MLA_Q_LORA, -1).astype(BF16)
    uk = mla_w_uk[l].reshape(MLA_KV_LORA, MLA_HEADS * MLA_NOPE).astype(BF16)
    uv = mla_w_uv[l].reshape(MLA_KV_LORA, MLA_HEADS * MLA_V).astype(BF16)
    w_r = jnp.concatenate([moe_w_group[l], moe_w_expert[l].reshape(d, N_EXPERTS),
                           jnp.zeros((d, ROUTER_W - N_GROUPS - N_EXPERTS), F32)], axis=1)
    b_r = jnp.concatenate([moe_b_group[l], moe_b_expert[l].reshape(N_EXPERTS),
                           jnp.zeros((ROUTER_W - N_GROUPS - N_EXPERTS,), F32)])[None, :]
    return w_in_p, w_out[l].astype(BF16), uq_p, uk, uv, w_r, b_r


def kernel(x_prompt, x_sample, cache_kv_latent, cache_k_rope, state_ret, state_conv, page_table, w_in, w_out, ret_gn_g, ret_gn_b, mla_q_norm_g, mla_kv_norm_g, mla_w_uq, mla_w_uk, mla_w_uv, conv_w_dw, conv_b_dw, conv_gn_g, conv_gn_b, ln1_g, ln1_b, ln2_g, ln2_b, moe_w_group, moe_b_group, moe_w_expert, moe_b_expert, moe_w1, moe_w3, moe_w2):
    batch, seq, d_model = x_prompt.shape
    db, dec_seq, _ = x_sample.shape
    assert dec_seq == 1, "the sample group is a single-token decode step"
    depth = w_in.shape[0]
    n_prompt = batch * seq
    n_tok = n_prompt + db
    past = page_table.shape[1] * cache_kv_latent.shape[2]

    pos_p = jnp.arange(seq, dtype=jnp.int32)
    pos_s = past + jnp.arange(dec_seq, dtype=jnp.int32)
    pos_all = jnp.concatenate([jnp.tile(pos_p, batch), jnp.tile(pos_s, db)])
    cos64, sin64 = _rope_tables(pos_all, MLA_ROPE)
    cos128_p, sin128_p = _rope_tables(pos_p, RET_DK)
    cos128_s, sin128_s = _rope_tables(pos_s, RET_DK)

    n_slots = 2 * n_tok
    n_tiles = (n_slots + N_EXPERTS * (EXPERT_TILE - 1)) // EXPERT_TILE + 1

    x = jnp.concatenate([x_prompt.reshape(n_prompt, d_model), x_sample.reshape(db, d_model)], axis=0)
    xb = x.astype(BF16)
    outs = [[] for _ in range(8)]
    for l in range(depth):
        w_in_p, w_out_b, uq_p, uk, uv, w_r, b_r = _prep_weights(
            l, w_in, w_out, mla_w_uq, mla_w_uk, mla_w_uv, moe_w_group, moe_b_group, moe_w_expert, moe_b_expert)
        h = _matmul(xb, w_in_p, F32, 1024, "in_proj")

        cqn, lat, latb, krope = _mla_prep(h, cos64, sin64, mla_q_norm_g[l][None], mla_kv_norm_g[l][None])
        q_full = _qproj(cqn, uq_p, cos64, sin64)
        k_full, v_full = _kvup(latb, krope, uk, uv, n_prompt)
        m_mla = _attn_prompt(q_full, k_full, v_full, batch, seq, n_tok)
        qlat = _qlat(q_full, uk, n_prompt, db)
        hp = 16
        qlat = jnp.pad(jnp.swapaxes(qlat, 0, 1), ((0, 0), (0, hp - MLA_HEADS), (0, 0)))
        qr = q_full[n_prompt:].reshape(db, MLA_HEADS, Q_HEAD_W)[:, :, MLA_NOPE:MLA_NOPE + MLA_ROPE]
        qr = jnp.pad((qr.astype(F32) * MLA_SCALE).astype(BF16), ((0, 0), (0, hp - MLA_HEADS), (0, 0)))
        ctx = _attn_sample(page_table, qlat, qr, lat[n_prompt:].reshape(db, 1, MLA_KV_LORA),
                           krope[n_prompt:].reshape(db, 1, LANES), cache_kv_latent, cache_k_rope, l)
        m_mla = _ctxproj(jnp.swapaxes(ctx[:, :MLA_HEADS], 0, 1), uv, m_mla, n_prompt)

        m_ret, s_ret_p = _ret_prompt(h, cos128_p, sin128_p, ret_gn_g[l][None], ret_gn_b[l][None], batch, seq, n_tok)
        m_ret, s_ret_s = _ret_sample(h, cos128_s, sin128_s, ret_gn_g[l][None], ret_gn_b[l][None], state_ret[l],
                                     m_ret, n_prompt)

        m_conv, buf_p = _conv_prompt(h, conv_w_dw[l], conv_b_dw[l][None], conv_gn_g[l][None], conv_gn_b[l][None],
                                     batch, seq, n_tok)
        m_conv, buf_s = _conv_sample(h, conv_w_dw[l], conv_b_dw[l][None], conv_gn_g[l][None], conv_gn_b[l][None],
                                     state_conv[l], m_conv, n_prompt)

        z = _outproj(m_ret, m_mla, m_conv, w_out_b, x)
        x1, route, counts = _ln_router(z, ln1_g[l][None], ln1_b[l][None], w_r, b_r)

        dest, tile_expert, n_active = _route_plan(route, counts, n_tiles)
        x_sorted = _scatter_rows(x1, dest, n_tiles * EXPERT_TILE)
        y_sorted = _experts(x_sorted, tile_expert, n_active, moe_w1[l], moe_w3[l], moe_w2[l])
        x, xb = _combine_ln(y_sorted, dest, x1, route, ln2_g[l][None], ln2_b[l][None])

        outs[0].append(lat[:n_prompt].reshape(batch, seq, MLA_KV_LORA))
        outs[1].append(krope[:n_prompt, :MLA_ROPE].reshape(batch, seq, MLA_ROPE))
        outs[2].append(s_ret_p)
        outs[3].append(buf_p)
        outs[4].append(lat[n_prompt:].reshape(db, dec_seq, MLA_KV_LORA))
        outs[5].append(krope[n_prompt:, :MLA_ROPE].reshape(db, dec_seq, MLA_ROPE))
        outs[6].append(s_ret_s)
        outs[7].append(buf_s)

    y_prompt = x[:n_prompt].reshape(batch, seq, d_model)
    y_sample = x[n_prompt:].reshape(db, dec_seq, d_model)
    return (y_prompt, y_sample) + tuple(jnp.stack(o) for o in outs)
```

```python
import functools

import jax
import jax.numpy as jnp
from jax import lax
from jax.experimental import pallas as pl
from jax.experimental.pallas import tpu as pltpu

F32 = jnp.float32
BF16 = jnp.bfloat16

RET_HEADS, RET_DK, RET_DV, RET_CHUNK = 6, 128, 256, 128
MLA_HEADS, MLA_Q_LORA, MLA_KV_LORA, MLA_NOPE, MLA_ROPE, MLA_V = 12, 768, 512, 128, 64, 128
MLA_SCALE = (MLA_NOPE + MLA_ROPE) ** -0.5
CONV_CH, CONV_WIDTH, CONV_GROUPS = 1024, 31, 8
N_GROUPS, EXPERTS_PER_GROUP, D_EXPERT = 4, 8, 256
N_EXPERTS = N_GROUPS * EXPERTS_PER_GROUP
ROPE_BASE = 10000.0
NORM_EPS = 1e-5
NEG_INF = -1e30
DEPTH = 2
DEEPNORM_ALPHA = (2 * DEPTH) ** 0.25

LANES = 128
BF16_ROWS = 16
VMEM_LIMIT_MAX = 56 * 1024 * 1024

H_RQ, H_RK, H_RV, H_RG = 0, 768, 1536, 3072
H_CQ, H_KR, H_CKV, H_CA, H_CG = 4608, 5376, 5632, 6144, 7168
H_KR_W = 256
H_TOT = 8192
Q_HEAD_W = 256
ROUTER_W = LANES
EXPERT_TILE = 128


def _cparams(semantics, vmem_bytes):
    return pltpu.CompilerParams(dimension_semantics=semantics,
                                vmem_limit_bytes=int(min(max(vmem_bytes, 16 * 1024 * 1024), VMEM_LIMIT_MAX)))


def _row_tile(n, candidates=(640, 512, 256, 128)):
    for c in candidates:
        if n % c == 0:
            return c
    raise ValueError(f"row count {n} is not a multiple of {candidates[-1]}")


def _rope_half(x, cos, sin, half):
    n = x.shape[-1]
    if 2 * half == n:
        swapped = pltpu.roll(x, half, axis=x.ndim - 1)
    else:
        lane = lax.broadcasted_iota(jnp.int32, x.shape, x.ndim - 1)
        swapped = jnp.where(lane < half, pltpu.roll(x, n - half, axis=x.ndim - 1),
                            pltpu.roll(x, half, axis=x.ndim - 1))
    return x * cos + swapped * sin


def _layer_norm_rows(x, g, b):
    mean = jnp.mean(x, axis=-1, keepdims=True)
    xc = x - mean
    var = jnp.mean(xc * xc, axis=-1, keepdims=True)
    return xc * lax.rsqrt(var + NORM_EPS) * g + b


def _silu(x):
    return x * jax.nn.sigmoid(x)


def _dot(a, b):
    return jnp.dot(a, b, preferred_element_type=F32)


def _dot_nt(a, b):
    return lax.dot_general(a, b, (((1,), (1,)), ((), ())), preferred_element_type=F32)


def _dot_tn(a, b):
    return lax.dot_general(a, b, (((0,), (0,)), ((), ())), preferred_element_type=F32)


def _mm_kernel(a_ref, b_ref, o_ref):
    o_ref[...] = _dot(a_ref[...], b_ref[...]).astype(o_ref.dtype)


def _matmul(a, b, out_dtype, tn, name):
    m, k = a.shape
    n = b.shape[1]
    tm = _row_tile(m)
    vmem = 2 * (tm * k * 2 + k * tn * 2 + tm * tn * 4) + tm * tn * 4 + (8 << 20)
    return pl.pallas_call(
        _mm_kernel, grid=(m // tm, n // tn),
        in_specs=[pl.BlockSpec((tm, k), lambda i, j: (i, 0)), pl.BlockSpec((k, tn), lambda i, j: (0, j))],
        out_specs=pl.BlockSpec((tm, tn), lambda i, j: (i, j)),
        out_shape=jax.ShapeDtypeStruct((m, n), out_dtype),
        compiler_params=_cparams(("parallel", "parallel"), vmem), name=name)(a, b)


def _mla_prep_kernel(cq_ref, kr_ref, ckv_ref, cos_ref, sin_ref, qg_ref, kvg_ref,
                     cqn_ref, lat_ref, latb_ref, krope_ref):
    cq = cq_ref[...]
    cqn = cq * lax.rsqrt(jnp.mean(cq * cq, axis=-1, keepdims=True) + NORM_EPS) * qg_ref[...]
    cqn_ref[...] = cqn.astype(BF16)
    ckv = ckv_ref[...]
    lat = ckv * lax.rsqrt(jnp.mean(ckv * ckv, axis=-1, keepdims=True) + NORM_EPS) * kvg_ref[...]
    lat_ref[...] = lat
    latb_ref[...] = lat.astype(BF16)
    kr = kr_ref[...][:, :LANES]
    krope_ref[...] = _rope_half(kr, cos_ref[...], sin_ref[...], MLA_ROPE // 2)


def _mla_prep(h, cos64, sin64, q_norm_g, kv_norm_g):
    n = h.shape[0]
    tm = _row_tile(n)
    return pl.pallas_call(
        _mla_prep_kernel, grid=(n // tm,),
        in_specs=[pl.BlockSpec((tm, MLA_Q_LORA), lambda i: (i, H_CQ // MLA_Q_LORA)),
                  pl.BlockSpec((tm, H_KR_W), lambda i: (i, H_KR // H_KR_W)),
                  pl.BlockSpec((tm, MLA_KV_LORA), lambda i: (i, H_CKV // MLA_KV_LORA)),
                  pl.BlockSpec((tm, LANES), lambda i: (i, 0)),
                  pl.BlockSpec((tm, LANES), lambda i: (i, 0)),
                  pl.BlockSpec((1, MLA_Q_LORA), lambda i: (0, 0)),
                  pl.BlockSpec((1, MLA_KV_LORA), lambda i: (0, 0))],
        out_specs=[pl.BlockSpec((tm, MLA_Q_LORA), lambda i: (i, 0)),
                   pl.BlockSpec((tm, MLA_KV_LORA), lambda i: (i, 0)),
                   pl.BlockSpec((tm, MLA_KV_LORA), lambda i: (i, 0)),
                   pl.BlockSpec((tm, LANES), lambda i: (i, 0))],
        out_shape=[jax.ShapeDtypeStruct((n, MLA_Q_LORA), BF16),
                   jax.ShapeDtypeStruct((n, MLA_KV_LORA), F32),
                   jax.ShapeDtypeStruct((n, MLA_KV_LORA), BF16),
                   jax.ShapeDtypeStruct((n, LANES), F32)],
        compiler_params=_cparams(("parallel",), 24 << 20), name="mla_prep",
    )(h, h, h, cos64, sin64, q_norm_g, kv_norm_g)


def _qproj_kernel(a_ref, w_ref, cos_ref, sin_ref, o_ref):
    acc = _dot(a_ref[...], w_ref[...])
    o_ref[:, :MLA_NOPE] = acc[:, :MLA_NOPE].astype(BF16)
    o_ref[:, MLA_NOPE:] = _rope_half(acc[:, MLA_NOPE:], cos_ref[...], sin_ref[...], MLA_ROPE // 2).astype(BF16)


def _qproj(cqn, w_uq, cos64, sin64):
    n = cqn.shape[0]
    tm = _row_tile(n)
    return pl.pallas_call(
        _qproj_kernel, grid=(n // tm, MLA_HEADS),
        in_specs=[pl.BlockSpec((tm, MLA_Q_LORA), lambda i, h: (i, 0)),
                  pl.BlockSpec((MLA_Q_LORA, Q_HEAD_W), lambda i, h: (0, h)),
                  pl.BlockSpec((tm, LANES), lambda i, h: (i, 0)),
                  pl.BlockSpec((tm, LANES), lambda i, h: (i, 0))],
        out_specs=pl.BlockSpec((tm, Q_HEAD_W), lambda i, h: (i, h)),
        out_shape=jax.ShapeDtypeStruct((n, MLA_HEADS * Q_HEAD_W), BF16),
        compiler_params=_cparams(("parallel", "parallel"), 16 << 20), name="mla_qproj",
    )(cqn, w_uq, cos64, sin64)


def _kvup_kernel(lat_ref, kr_ref, wk_ref, wv_ref, k_ref, v_ref):
    lat = lat_ref[...]
    kn = _dot(lat, wk_ref[...])
    krb = kr_ref[...].astype(BF16)
    for hh in range(MLA_HEADS):
        k_ref[:, hh * Q_HEAD_W:hh * Q_HEAD_W + MLA_NOPE] = kn[:, hh * MLA_NOPE:(hh + 1) * MLA_NOPE].astype(BF16)
        k_ref[:, hh * Q_HEAD_W + MLA_NOPE:(hh + 1) * Q_HEAD_W] = krb
    v_ref[...] = _dot(lat, wv_ref[...]).astype(BF16)


def _kvup(latb, krope, w_uk, w_uv, n_prompt):
    tm = _row_tile(n_prompt, (512, 256, 128))
    wn = MLA_HEADS * MLA_NOPE
    return pl.pallas_call(
        _kvup_kernel, grid=(n_prompt // tm,),
        in_specs=[pl.BlockSpec((tm, MLA_KV_LORA), lambda i: (i, 0)),
                  pl.BlockSpec((tm, LANES), lambda i: (i, 0)),
                  pl.BlockSpec((MLA_KV_LORA, wn), lambda i: (0, 0)),
                  pl.BlockSpec((MLA_KV_LORA, MLA_HEADS * MLA_V), lambda i: (0, 0))],
        out_specs=[pl.BlockSpec((tm, MLA_HEADS * Q_HEAD_W), lambda i: (i, 0)),
                   pl.BlockSpec((tm, MLA_HEADS * MLA_V), lambda i: (i, 0))],
        out_shape=[jax.ShapeDtypeStruct((n_prompt, MLA_HEADS * Q_HEAD_W), BF16),
                   jax.ShapeDtypeStruct((n_prompt, MLA_HEADS * MLA_V), BF16)],
        compiler_params=_cparams(("parallel",), 32 << 20), name="mla_kvup",
    )(latb, krope, w_uk, w_uv)


def _attn_p_kernel(q_ref, k_ref, v_ref, init_ref, o_ref, *, tq, seq):
    del init_ref
    row = lax.broadcasted_iota(jnp.int32, (tq, tq), 0)
    col = lax.broadcasted_iota(jnp.int32, (tq, tq), 1)
    for i in range(seq // tq):
        lo, hi = i * tq, (i + 1) * tq
        q = q_ref[lo:hi, :]
        sd = jnp.where(col <= row, _dot_nt(q, k_ref[lo:hi, :]) * MLA_SCALE, NEG_INF)
        m = jnp.max(sd, axis=-1, keepdims=True)
        if i:
            sl = _dot_nt(q, k_ref[0:lo, :]) * MLA_SCALE
            m = jnp.maximum(m, jnp.max(sl, axis=-1, keepdims=True))
        pd = jnp.exp(sd - m)
        l = jnp.sum(pd, axis=-1, keepdims=True)
        acc = _dot(pd.astype(BF16), v_ref[lo:hi, :])
        if i:
            pl_ = jnp.exp(sl - m)
            l = l + jnp.sum(pl_, axis=-1, keepdims=True)
            acc = acc + _dot(pl_.astype(BF16), v_ref[0:lo, :])
        o_ref[lo:hi, :] = (acc / l).astype(o_ref.dtype)


def _attn_prompt(q_full, k_full, v_full, batch, seq, n_tok):
    tq = 256 if seq % 256 == 0 else 128
    kern = functools.partial(_attn_p_kernel, tq=tq, seq=seq)
    return pl.pallas_call(
        kern, grid=(batch, MLA_HEADS),
        in_specs=[pl.BlockSpec((seq, Q_HEAD_W), lambda b, h: (b, h)),
                  pl.BlockSpec((seq, Q_HEAD_W), lambda b, h: (b, h)),
                  pl.BlockSpec((seq, MLA_V), lambda b, h: (b, h)),
                  pl.BlockSpec(memory_space=pl.ANY)],
        out_specs=pl.BlockSpec((seq, MLA_V), lambda b, h: (b, h)),
        out_shape=jax.ShapeDtypeStruct((n_tok, MLA_HEADS * MLA_V), BF16),
        input_output_aliases={3: 0},
        compiler_params=_cparams(("parallel", "parallel"), 32 << 20), name="mla_attn_prompt",
    )(q_full, k_full, v_full, jnp.zeros((n_tok, MLA_HEADS * MLA_V), BF16))


def _qlat_kernel(q_ref, wk_ref, o_ref):
    qn = q_ref[...][:, :MLA_NOPE]
    o_ref[0] = (_dot_nt(qn, wk_ref[...]) * MLA_SCALE).astype(BF16)


def _qlat(q_full, w_uk, n_prompt, n_sample):
    rb = n_prompt // n_sample
    return pl.pallas_call(
        _qlat_kernel, grid=(MLA_HEADS,),
        in_specs=[pl.BlockSpec((n_sample, Q_HEAD_W), lambda h: (rb, h)),
                  pl.BlockSpec((MLA_KV_LORA, MLA_NOPE), lambda h: (0, h))],
        out_specs=pl.BlockSpec((1, n_sample, MLA_KV_LORA), lambda h: (h, 0, 0)),
        out_shape=jax.ShapeDtypeStruct((MLA_HEADS, n_sample, MLA_KV_LORA), BF16),
        compiler_params=_cparams(("parallel",), 16 << 20), name="mla_qlat",
    )(q_full, w_uk)


def _attn_s_kernel(pt_ref, qlat_ref, qr_ref, lat_ref, kr_ref, ckv_ref, ckr_ref, o_ref,
                   kvbuf, krbuf, kv_sc, kr_sc, sem, m_sc, l_sc, acc_sc, *, pps, n_steps, total_steps, page):
    step = pl.program_id(1)
    lin = pl.program_id(0) * n_steps + step
    slot = lax.rem(lin, 2)

    def page_copies(lin_idx, slot_idx, lookup):
        cps = []
        for i in range(pps):
            row = pt_ref[lin_idx * pps + i] if lookup else 0
            cps.append(pltpu.make_async_copy(ckv_ref.at[row], kvbuf.at[slot_idx, pl.ds(i * page, page), :],
                                             sem.at[slot_idx]))
            cps.append(pltpu.make_async_copy(ckr_ref.at[row], krbuf.at[slot_idx, :, pl.ds(i * page, page)],
                                             sem.at[slot_idx]))
        return cps

    @pl.when(lin == 0)
    def _():
        for cp in page_copies(0, 0, True):
            cp.start()

    @pl.when(lin + 1 < total_steps)
    def _():
        for cp in page_copies(lin + 1, 1 - slot, True):
            cp.start()

    for cp in page_copies(lin, slot, False):
        cp.wait()

    @pl.when(step == 0)
    def _():
        m_sc[...] = jnp.full(m_sc.shape, NEG_INF, F32)
        l_sc[...] = jnp.zeros(l_sc.shape, F32)
        acc_sc[...] = jnp.zeros(acc_sc.shape, F32)

    kv_sc[...] = kvbuf[slot].astype(BF16)
    kr_sc[...] = krbuf[slot].astype(BF16)
    qlat = qlat_ref[0]
    qr = qr_ref[0]
    kv = kv_sc[...]
    s = _dot_nt(qlat, kv) + _dot(qr, kr_sc[...])
    m_prev = m_sc[...]
    m_new = jnp.maximum(m_prev, jnp.max(s, axis=-1, keepdims=True))
    corr = jnp.exp(m_prev - m_new)
    p = jnp.exp(s - m_new)
    l = l_sc[...] * corr + jnp.sum(p, axis=-1, keepdims=True)
    acc = acc_sc[...] * corr + _dot(p.astype(BF16), kv)
    m_sc[...] = m_new
    l_sc[...] = l
    acc_sc[...] = acc

    @pl.when(step == n_steps - 1)
    def _():
        lat = lat_ref[0]
        kr = kr_ref[0][:, :MLA_ROPE]
        s_own = (jnp.sum(qlat.astype(F32) * lat, axis=-1, keepdims=True)
                 + jnp.sum(qr.astype(F32) * kr, axis=-1, keepdims=True))
        m_fin = jnp.maximum(m_new, s_own)
        c = jnp.exp(m_new - m_fin)
        p_own = jnp.exp(s_own - m_fin)
        o_ref[0] = (acc * c + p_own * lat) / (l * c + p_own)


def _attn_sample(page_rows, qlat, qr, lat_s, kr_s, cache_kv, cache_kr_t):
    db, n_pages = page_rows.shape
    page = cache_kv.shape[1]
    pps = 16
    while n_pages % pps:
        pps //= 2
    n_steps = n_pages // pps
    hp = qlat.shape[1]
    rows = pps * page
    kern = functools.partial(_attn_s_kernel, pps=pps, n_steps=n_steps, total_steps=db * n_steps, page=page)
    in_specs = [pl.BlockSpec((1, hp, MLA_KV_LORA), lambda b, s, pt: (b, 0, 0)),
                pl.BlockSpec((1, hp, MLA_ROPE), lambda b, s, pt: (b, 0, 0)),
                pl.BlockSpec((1, 1, MLA_KV_LORA), lambda b, s, pt: (b, 0, 0)),
                pl.BlockSpec((1, 1, LANES), lambda b, s, pt: (b, 0, 0)),
                pl.BlockSpec(memory_space=pl.ANY), pl.BlockSpec(memory_space=pl.ANY)]
    vmem = 2 * rows * (MLA_KV_LORA + MLA_ROPE) * 4 + 3 * rows * (MLA_KV_LORA + MLA_ROPE) * 2 + (8 << 20)
    return pl.pallas_call(
        kern,
        grid_spec=pltpu.PrefetchScalarGridSpec(
            num_scalar_prefetch=1, grid=(db, n_steps), in_specs=in_specs,
            out_specs=pl.BlockSpec((1, hp, MLA_KV_LORA), lambda b, s, pt: (b, 0, 0)),
            scratch_shapes=[pltpu.VMEM((2, rows, MLA_KV_LORA), F32), pltpu.VMEM((2, MLA_ROPE, rows), F32),
                            pltpu.VMEM((rows, MLA_KV_LORA), BF16), pltpu.VMEM((MLA_ROPE, rows), BF16),
                            pltpu.SemaphoreType.DMA((2,)),
                            pltpu.VMEM((hp, 1), F32), pltpu.VMEM((hp, 1), F32),
                            pltpu.VMEM((hp, MLA_KV_LORA), F32)]),
        out_shape=jax.ShapeDtypeStruct((db, hp, MLA_KV_LORA), F32),
        compiler_params=_cparams(("arbitrary", "arbitrary"), vmem), name="mla_attn_sample",
    )(page_rows.reshape(-1), qlat, qr, lat_s, kr_s, cache_kv, cache_kr_t)


def _ctxproj_kernel(ctx_ref, wv_ref, mixed_ref, o_ref):
    del mixed_ref
    o_ref[...] = _dot(ctx_ref[0].astype(BF16), wv_ref[...]).astype(o_ref.dtype)


def _ctxproj(ctx_t, w_uv, mixed_mla, n_prompt):
    n_sample = ctx_t.shape[1]
    rb = n_prompt // n_sample
    return pl.pallas_call(
        _ctxproj_kernel, grid=(MLA_HEADS,),
        in_specs=[pl.BlockSpec((1, n_sample, MLA_KV_LORA), lambda h: (h, 0, 0)),
                  pl.BlockSpec((MLA_KV_LORA, MLA_V), lambda h: (0, h)),
                  pl.BlockSpec(memory_space=pl.ANY)],
        out_specs=pl.BlockSpec((n_sample, MLA_V), lambda h: (rb, h)),
        out_shape=jax.ShapeDtypeStruct(mixed_mla.shape, mixed_mla.dtype),
        input_output_aliases={2: 0},
        compiler_params=_cparams(("parallel",), 16 << 20), name="mla_ctxproj",
    )(ctx_t, w_uv, mixed_mla)


def _ret_p_kernel(rq_ref, rk_ref, rv_ref, rg_ref, cos_ref, sin_ref, decay_ref, cross_ref, kdec_ref, gc_ref,
                  gng_ref, gnb_ref, init_ref, o_ref, s_out_ref, s_sc, *, n_chunks):
    del init_ref
    c = pl.program_id(1)

    @pl.when(c == 0)
    def _():
        s_sc[...] = jnp.zeros(s_sc.shape, F32)

    cos, sin = cos_ref[...], sin_ref[...]
    for hh in range(RET_HEADS):
        q = _rope_half(rq_ref[:, hh * RET_DK:(hh + 1) * RET_DK], cos, sin, RET_DK // 2)
        k = _rope_half(rk_ref[:, hh * RET_DK:(hh + 1) * RET_DK], cos, sin, RET_DK // 2) * (RET_DK ** -0.5)
        v = rv_ref[:, hh * RET_DV:(hh + 1) * RET_DV].astype(BF16)
        qb = q.astype(BF16)
        s_old = s_sc[hh]
        scores = _dot_nt(qb, k.astype(BF16)) * decay_ref[hh]
        o = _dot(scores.astype(BF16), v) + _dot(qb, s_old.astype(BF16)) * cross_ref[hh]
        kd = (k * kdec_ref[hh]).astype(BF16)
        s_sc[hh] = gc_ref[hh] * s_old + _dot_tn(kd, v)
        sl = slice(hh * RET_DV, (hh + 1) * RET_DV)
        on = _layer_norm_rows(o, gng_ref[:, sl], gnb_ref[:, sl])
        o_ref[:, sl] = (_silu(rg_ref[:, sl]) * on).astype(o_ref.dtype)

    @pl.when(c == n_chunks - 1)
    def _():
        s_out_ref[0] = s_sc[...]


def _ret_consts(chunk):
    hidx = jnp.arange(RET_HEADS, dtype=F32)
    log_g = jnp.log1p(-jnp.exp2(-5.0 - hidx))
    idx = jnp.arange(chunk, dtype=F32)
    diff = idx[:, None] - idx[None, :]
    decay = jnp.where(diff >= 0, jnp.exp(log_g[:, None, None] * jnp.maximum(diff, 0.0)[None]), 0.0)
    cross = jnp.exp((idx[:, None] + 1.0) * log_g[None, :]).T
    kdec = jnp.exp((chunk - 1.0 - idx)[:, None] * log_g[None, :]).T
    gc = jnp.exp(chunk * log_g)
    return log_g, decay, cross, kdec, gc


def _ret_prompt(h, cos128, sin128, gn_g, gn_b, batch, seq, n_tok):
    chunk = RET_CHUNK if seq % RET_CHUNK == 0 else seq
    n_chunks = seq // chunk
    _, decay, cross, kdec, gc = _ret_consts(chunk)
    cross_b = jnp.broadcast_to(cross[:, :, None], (RET_HEADS, chunk, RET_DV))
    kdec_b = jnp.broadcast_to(kdec[:, :, None], (RET_HEADS, chunk, RET_DK))
    gc_b = jnp.broadcast_to(gc[:, None, None], (RET_HEADS, RET_DK, RET_DV))
    qw, vw = RET_HEADS * RET_DK, RET_HEADS * RET_DV
    kern = functools.partial(_ret_p_kernel, n_chunks=n_chunks)
    rows = lambda b, c: b * n_chunks + c
    const3 = lambda b, c: (0, 0, 0)
    return pl.pallas_call(
        kern, grid=(batch, n_chunks),
        in_specs=[pl.BlockSpec((chunk, qw), lambda b, c: (rows(b, c), H_RQ // qw)),
                  pl.BlockSpec((chunk, qw), lambda b, c: (rows(b, c), H_RK // qw)),
                  pl.BlockSpec((chunk, vw), lambda b, c: (rows(b, c), H_RV // vw)),
                  pl.BlockSpec((chunk, vw), lambda b, c: (rows(b, c), H_RG // vw)),
                  pl.BlockSpec((chunk, RET_DK), lambda b, c: (c, 0)),
                  pl.BlockSpec((chunk, RET_DK), lambda b, c: (c, 0)),
                  pl.BlockSpec((RET_HEADS, chunk, chunk), const3),
                  pl.BlockSpec((RET_HEADS, chunk, RET_DV), const3),
                  pl.BlockSpec((RET_HEADS, chunk, RET_DK), const3),
                  pl.BlockSpec((RET_HEADS, RET_DK, RET_DV), const3),
                  pl.BlockSpec((1, vw), lambda b, c: (0, 0)),
                  pl.BlockSpec((1, vw), lambda b, c: (0, 0)),
                  pl.BlockSpec(memory_space=pl.ANY)],
        out_specs=[pl.BlockSpec((chunk, vw), lambda b, c: (rows(b, c), 0)),
                   pl.BlockSpec((1, RET_HEADS, RET_DK, RET_DV), lambda b, c: (b, 0, 0, 0))],
        out_shape=[jax.ShapeDtypeStruct((n_tok, vw), BF16),
                   jax.ShapeDtypeStruct((batch, RET_HEADS, RET_DK, RET_DV), F32)],
        input_output_aliases={12: 0},
        scratch_shapes=[pltpu.VMEM((RET_HEADS, RET_DK, RET_DV), F32)],
        compiler_params=_cparams(("parallel", "arbitrary"), 32 << 20), name="ret_prompt",
    )(h, h, h, h, cos128, sin128, decay, cross_b, kdec_b, gc_b, gn_g, gn_b, jnp.zeros((n_tok, vw), BF16))


def _ret_s_kernel(rq_ref, rk_ref, rv_ref, rg_ref, cos_ref, sin_ref, g_ref, gng_ref, gnb_ref, s_ref, mixed_ref,
                  o_ref, s_out_ref, *, bb):
    del mixed_ref
    cos, sin = cos_ref[...], sin_ref[...]
    q = _rope_half(rq_ref[...], cos, sin, RET_DK // 2)
    k = _rope_half(rk_ref[...], cos, sin, RET_DK // 2) * (RET_DK ** -0.5)
    qk = jnp.sum(q * k, axis=-1, keepdims=True)
    qt, kt = q.T, k.T
    v = rv_ref[...]
    g = g_ref[0]
    rows = []
    for b in range(bb):
        s_old = s_ref[b]
        vb = v[b:b + 1, :]
        qs = jnp.sum(qt[:, b:b + 1] * s_old, axis=0, keepdims=True)
        rows.append(qk[b:b + 1, :] * vb + qs * g)
        s_out_ref[b] = g * s_old + kt[:, b:b + 1] * vb
    o = jnp.concatenate(rows, axis=0)
    on = _layer_norm_rows(o, gng_ref[...], gnb_ref[...])
    o_ref[...] = (_silu(rg_ref[...]) * on).astype(o_ref.dtype)


def _ret_sample(h, cos_row, sin_row, gn_g, gn_b, state, layer, mixed_ret, n_prompt):
    db = state.shape[1]
    bb = BF16_ROWS
    log_g = _ret_consts(1)[0]
    g_b = jnp.broadcast_to(jnp.exp(log_g)[:, None, None], (RET_HEADS, 1, RET_DV))
    rb = n_prompt // bb
    kern = functools.partial(_ret_s_kernel, bb=bb)
    return pl.pallas_call(
        kern, grid=(db // bb, RET_HEADS),
        in_specs=[pl.BlockSpec((bb, RET_DK), lambda i, hh: (rb + i, H_RQ // RET_DK + hh)),
                  pl.BlockSpec((bb, RET_DK), lambda i, hh: (rb + i, H_RK // RET_DK + hh)),
                  pl.BlockSpec((bb, RET_DV), lambda i, hh: (rb + i, H_RV // RET_DV + hh)),
                  pl.BlockSpec((bb, RET_DV), lambda i, hh: (rb + i, H_RG // RET_DV + hh)),
                  pl.BlockSpec((1, RET_DK), lambda i, hh: (0, 0)),
                  pl.BlockSpec((1, RET_DK), lambda i, hh: (0, 0)),
                  pl.BlockSpec((1, 1, RET_DV), lambda i, hh: (hh, 0, 0)),
                  pl.BlockSpec((1, RET_DV), lambda i, hh: (0, hh)),
                  pl.BlockSpec((1, RET_DV), lambda i, hh: (0, hh)),
                  pl.BlockSpec((None, bb, None, RET_DK, RET_DV), lambda i, hh: (layer, i, hh, 0, 0)),
                  pl.BlockSpec(memory_space=pl.ANY)],
        out_specs=[pl.BlockSpec((bb, RET_DV), lambda i, hh: (rb + i, hh)),
                   pl.BlockSpec((bb, None, RET_DK, RET_DV), lambda i, hh: (i, hh, 0, 0))],
        out_shape=[jax.ShapeDtypeStruct(mixed_ret.shape, mixed_ret.dtype),
                   jax.ShapeDtypeStruct(state.shape[1:], F32)],
        input_output_aliases={10: 0},
        compiler_params=_cparams(("parallel", "parallel"), 4 * bb * RET_DK * RET_DV * 4 + (8 << 20)),
        name="ret_sample",
    )(h, h, h, h, cos_row, sin_row, g_b, gn_g, gn_b, state, mixed_ret)


HIST = 32


def _conv_p_kernel(a_ref, g_ref, w_ref, bdw_ref, gng_ref, gnb_ref, init_ref, o_ref, buf_ref, u_sc, *, tt, n_t, rc):
    del init_ref
    t = pl.program_id(1)

    @pl.when(t == 0)
    def _():
        u_sc[0:HIST, :] = jnp.zeros((HIST, CONV_CH), F32)

    @pl.when(t > 0)
    def _():
        u_sc[0:HIST, :] = u_sc[tt:tt + HIST, :]

    u_sc[HIST:HIST + tt, :] = a_ref[...] * jax.nn.sigmoid(g_ref[...])
    off = HIST - (CONV_WIDTH - 1)
    gsz = CONV_CH // CONV_GROUPS
    for r0 in range(0, tt, rc):
        for gi in range(CONV_GROUPS):
            sl = slice(gi * gsz, (gi + 1) * gsz)
            acc = jnp.zeros((rc, gsz), F32)
            for w in range(CONV_WIDTH):
                acc = acc + u_sc[r0 + off + w:r0 + off + w + rc, sl] * w_ref[w:w + 1, sl]
            y = acc + bdw_ref[:, sl]
            o_ref[r0:r0 + rc, sl] = _silu(_layer_norm_rows(y, gng_ref[:, sl], gnb_ref[:, sl])).astype(o_ref.dtype)

    @pl.when(t == n_t - 1)
    def _():
        buf_ref[0] = u_sc[HIST + tt - (CONV_WIDTH - 1):HIST + tt, :]


def _conv_prompt(h, w_dw, b_dw, gn_g, gn_b, batch, seq, n_tok):
    tt = 256 if seq % 256 == 0 else 128
    n_t = seq // tt
    kern = functools.partial(_conv_p_kernel, tt=tt, n_t=n_t, rc=64)
    rows = lambda b, t: b * n_t + t
    vec = lambda b, t: (0, 0)
    return pl.pallas_call(
        kern, grid=(batch, n_t),
        in_specs=[pl.BlockSpec((tt, CONV_CH), lambda b, t: (rows(b, t), H_CA // CONV_CH)),
                  pl.BlockSpec((tt, CONV_CH), lambda b, t: (rows(b, t), H_CG // CONV_CH)),
                  pl.BlockSpec((CONV_WIDTH, CONV_CH), vec),
                  pl.BlockSpec((1, CONV_CH), vec), pl.BlockSpec((1, CONV_CH), vec), pl.BlockSpec((1, CONV_CH), vec),
                  pl.BlockSpec(memory_space=pl.ANY)],
        out_specs=[pl.BlockSpec((tt, CONV_CH), lambda b, t: (rows(b, t), 0)),
                   pl.BlockSpec((1, CONV_WIDTH - 1, CONV_CH), lambda b, t: (b, 0, 0))],
        out_shape=[jax.ShapeDtypeStruct((n_tok, CONV_CH), BF16),
                   jax.ShapeDtypeStruct((batch, CONV_WIDTH - 1, CONV_CH), F32)],
        input_output_aliases={6: 0},
        scratch_shapes=[pltpu.VMEM((HIST + tt, CONV_CH), F32)],
        compiler_params=_cparams(("parallel", "arbitrary"), 24 << 20), name="conv_prompt",
    )(h, h, w_dw, b_dw, gn_g, gn_b, jnp.zeros((n_tok, CONV_CH), BF16))


def _conv_s_kernel(a_ref, g_ref, w_ref, bdw_ref, gng_ref, gnb_ref, buf_ref, mixed_ref, o_ref, nbuf_ref):
    del mixed_ref
    hist = CONV_WIDTH - 1
    u = a_ref[...] * jax.nn.sigmoid(g_ref[...])
    y = u * w_ref[hist:hist + 1, :] + bdw_ref[...]
    for w in range(hist):
        slab = buf_ref[w]
        y = y + slab * w_ref[w:w + 1, :]
        if w:
            nbuf_ref[w - 1] = slab
    nbuf_ref[hist - 1] = u
    gsz = CONV_CH // CONV_GROUPS
    for gi in range(CONV_GROUPS):
        sl = slice(gi * gsz, (gi + 1) * gsz)
        o_ref[:, sl] = _silu(_layer_norm_rows(y[:, sl], gng_ref[:, sl], gnb_ref[:, sl])).astype(o_ref.dtype)


def _conv_sample(h, w_dw, b_dw, gn_g, gn_b, state_t, layer, mixed_conv, n_prompt):
    _, hist, db, _ = state_t.shape
    bb = BF16_ROWS
    rb = n_prompt // bb
    vec = lambda i: (0, 0)
    return pl.pallas_call(
        _conv_s_kernel, grid=(db // bb,),
        in_specs=[pl.BlockSpec((bb, CONV_CH), lambda i: (rb + i, H_CA // CONV_CH)),
                  pl.BlockSpec((bb, CONV_CH), lambda i: (rb + i, H_CG // CONV_CH)),
                  pl.BlockSpec((CONV_WIDTH, CONV_CH), vec),
                  pl.BlockSpec((1, CONV_CH), vec), pl.BlockSpec((1, CONV_CH), vec), pl.BlockSpec((1, CONV_CH), vec),
                  pl.BlockSpec((None, hist, bb, CONV_CH), lambda i: (layer, 0, i, 0)),
                  pl.BlockSpec(memory_space=pl.ANY)],
        out_specs=[pl.BlockSpec((bb, CONV_CH), lambda i: (rb + i, 0)),
                   pl.BlockSpec((hist, bb, CONV_CH), lambda i: (0, i, 0))],
        out_shape=[jax.ShapeDtypeStruct(mixed_conv.shape, mixed_conv.dtype),
                   jax.ShapeDtypeStruct(state_t.shape[1:], F32)],
        input_output_aliases={7: 0},
        compiler_params=_cparams(("parallel",), 24 << 20), name="conv_sample",
    )(h, h, w_dw, b_dw, gn_g, gn_b, state_t, mixed_conv)


def _outproj_kernel(a1_ref, a2_ref, a3_ref, w1_ref, w2_ref, w3_ref, x_ref, o_ref):
    z = _dot(a1_ref[...], w1_ref[...]) + _dot(a2_ref[...], w2_ref[...]) + _dot(a3_ref[...], w3_ref[...])
    o_ref[...] = DEEPNORM_ALPHA * x_ref[...] + z


def _outproj(m_ret, m_mla, m_conv, w_out, x):
    n, d = x.shape
    tm = _row_tile(n)
    tn = 1024
    k1, k2, k3 = m_ret.shape[1], m_mla.shape[1], m_conv.shape[1]
    assert k1 == k2 and (k1 + k2) % k3 == 0
    vmem = 2 * (tm * (k1 + k2 + k3) * 2 + (k1 + k2 + k3) * tn * 2 + 2 * tm * tn * 4) + (4 << 20)
    return pl.pallas_call(
        _outproj_kernel, grid=(n // tm, d // tn),
        in_specs=[pl.BlockSpec((tm, k1), lambda i, j: (i, 0)),
                  pl.BlockSpec((tm, k2), lambda i, j: (i, 0)),
                  pl.BlockSpec((tm, k3), lambda i, j: (i, 0)),
                  pl.BlockSpec((k1, tn), lambda i, j: (0, j)),
                  pl.BlockSpec((k2, tn), lambda i, j: (1, j)),
                  pl.BlockSpec((k3, tn), lambda i, j: ((k1 + k2) // k3, j)),
                  pl.BlockSpec((tm, tn), lambda i, j: (i, j))],
        out_specs=pl.BlockSpec((tm, tn), lambda i, j: (i, j)),
        out_shape=jax.ShapeDtypeStruct((n, d), F32),
        compiler_params=_cparams(("parallel", "parallel"), vmem), name="outproj",
    )(m_ret, m_mla, m_conv, w_out, w_out, w_out, x)


def _ln_router_kernel(z_ref, g_ref, b_ref, wr_ref, br_ref, x_ref, r_ref, cnt_ref, run_sc, *, n_t):
    t = pl.program_id(0)

    @pl.when(t == 0)
    def _():
        run_sc[...] = jnp.zeros(run_sc.shape, F32)

    x = _layer_norm_rows(z_ref[...], g_ref[...], b_ref[...])
    x_ref[...] = x
    tm = x.shape[0]
    logits = jnp.dot(x, wr_ref[...], preferred_element_type=F32, precision=lax.Precision.HIGHEST) + br_ref[...]
    lane = lax.broadcasted_iota(jnp.int32, (tm, ROUTER_W), 1).astype(F32)
    gl = jnp.where(lane < N_GROUPS, logits, -jnp.inf)
    gmax = jnp.max(gl, axis=-1, keepdims=True)
    g_p = 1.0 / jnp.sum(jnp.exp(gl - gmax), axis=-1, keepdims=True)
    g_idx = jnp.min(jnp.where(gl == gmax, lane, float(ROUTER_W)), axis=-1, keepdims=True)
    lo = N_GROUPS + g_idx * EXPERTS_PER_GROUP
    el = jnp.where(lane >= lo, jnp.where(lane < lo + EXPERTS_PER_GROUP, logits, -jnp.inf), -jnp.inf)
    e1 = jnp.max(el, axis=-1, keepdims=True)
    i1 = jnp.min(jnp.where(el == e1, lane, float(ROUTER_W)), axis=-1, keepdims=True)
    el2 = jnp.where(lane == i1, -jnp.inf, el)
    e2 = jnp.max(el2, axis=-1, keepdims=True)
    i2 = jnp.min(jnp.where(el2 == e2, lane, float(ROUTER_W)), axis=-1, keepdims=True)
    ex = jnp.exp(e2 - e1)
    w1 = g_p / (1.0 + ex)
    w2 = g_p * ex / (1.0 + ex)
    onehot = jnp.where(lane == i1, 1.0, jnp.where(lane == i2, 1.0, 0.0))
    tri = jnp.where(lax.broadcasted_iota(jnp.int32, (tm, tm), 1) < lax.broadcasted_iota(jnp.int32, (tm, tm), 0),
                    1.0, 0.0)
    before = _dot(tri.astype(BF16), onehot.astype(BF16)) + run_sc[...]
    rank1 = jnp.sum(jnp.where(lane == i1, before, 0.0), axis=-1, keepdims=True)
    rank2 = jnp.sum(jnp.where(lane == i2, before, 0.0), axis=-1, keepdims=True)
    run_sc[...] = run_sc[...] + jnp.sum(onehot, axis=0, keepdims=True)
    out = jnp.where(lane == 0, i1 - N_GROUPS, 0.0)
    out = jnp.where(lane == 1, i2 - N_GROUPS, out)
    out = jnp.where(lane == 2, w1, out)
    out = jnp.where(lane == 3, w2, out)
    out = jnp.where(lane == 4, rank1, out)
    out = jnp.where(lane == 5, rank2, out)
    r_ref[...] = out

    @pl.when(t == n_t - 1)
    def _():
        cnt_ref[...] = jnp.broadcast_to(run_sc[...], cnt_ref.shape)


def _ln_router(z, g, b, w_r, b_r):
    n, d = z.shape
    tm = _row_tile(n, (256, 128))
    n_t = n // tm
    kern = functools.partial(_ln_router_kernel, n_t=n_t)
    vec = lambda t: (0, 0)
    return pl.pallas_call(
        kern, grid=(n_t,),
        in_specs=[pl.BlockSpec((tm, d), lambda t: (t, 0)),
                  pl.BlockSpec((1, d), vec), pl.BlockSpec((1, d), vec),
                  pl.BlockSpec((d, ROUTER_W), vec), pl.BlockSpec((1, ROUTER_W), vec)],
        out_specs=[pl.BlockSpec((tm, d), lambda t: (t, 0)),
                   pl.BlockSpec((tm, ROUTER_W), lambda t: (t, 0)),
                   pl.BlockSpec((8, ROUTER_W), vec)],
        out_shape=[jax.ShapeDtypeStruct((n, d), F32),
                   jax.ShapeDtypeStruct((n, ROUTER_W), F32),
                   jax.ShapeDtypeStruct((8, ROUTER_W), F32)],
        scratch_shapes=[pltpu.VMEM((1, ROUTER_W), F32)],
        compiler_params=_cparams(("arbitrary",), 32 << 20), name="ln1_router",
    )(z, g, b, w_r, b_r)


def _scatter_kernel(dest_ref, x_ref, init_ref, o_ref, sem, *, tm):
    del init_ref
    t = pl.program_id(0)

    def copies(r):
        row = t * tm + r
        src = x_ref.at[pl.ds(r, 1)]
        return (pltpu.make_async_copy(src, o_ref.at[pl.ds(dest_ref[2 * row], 1)], sem),
                pltpu.make_async_copy(src, o_ref.at[pl.ds(dest_ref[2 * row + 1], 1)], sem))

    def start(r, c):
        for cp in copies(r):
            cp.start()
        return c

    def wait(r, c):
        for cp in copies(r):
            cp.wait()
        return c

    lax.fori_loop(0, tm, start, 0)
    lax.fori_loop(0, tm, wait, 0)


def _scatter_rows(x, dest_flat, n_rows_out):
    n, d = x.shape
    tm = 128
    kern = functools.partial(_scatter_kernel, tm=tm)
    init = jnp.zeros((n_rows_out, d), x.dtype)
    return pl.pallas_call(
        kern,
        grid_spec=pltpu.PrefetchScalarGridSpec(
            num_scalar_prefetch=1, grid=(n // tm,),
            in_specs=[pl.BlockSpec((tm, d), lambda t, dest: (t, 0)), pl.BlockSpec(memory_space=pl.ANY)],
            out_specs=pl.BlockSpec(memory_space=pl.ANY),
            scratch_shapes=[pltpu.SemaphoreType.DMA(())]),
        out_shape=jax.ShapeDtypeStruct((n_rows_out, d), x.dtype),
        input_output_aliases={2: 0},
        compiler_params=_cparams(("arbitrary",), 16 << 20), name="moe_scatter",
    )(dest_flat, x, init)


def _expert_kernel(te_ref, na_ref, x_ref, w1_ref, w3_ref, w2_ref, o_ref, w1_sc, w3_sc, w2_sc):
    t = pl.program_id(0)
    prev = te_ref[jnp.maximum(t - 1, 0)]
    fresh = (t == 0) | (te_ref[t] != prev)

    @pl.when(fresh & (t < na_ref[0]))
    def _():
        w1_sc[...] = w1_ref[...].astype(BF16)
        w3_sc[...] = w3_ref[...].astype(BF16)
        w2_sc[...] = w2_ref[...].astype(BF16)

    @pl.when(t < na_ref[0])
    def _():
        xb = x_ref[...].astype(BF16)
        hmid = _silu(_dot(xb, w1_sc[...])) * _dot(xb, w3_sc[...])
        o_ref[...] = _dot(hmid.astype(BF16), w2_sc[...])

    @pl.when(t >= na_ref[0])
    def _():
        o_ref[...] = jnp.zeros(o_ref.shape, o_ref.dtype)


def _experts(x_sorted, tile_expert, n_active, w1, w3, w2, layer):
    p, d = x_sorted.shape
    n_tiles = p // EXPERT_TILE
    f = w1.shape[3]

    def row_map(t, te, na):
        return (jnp.minimum(t, na[0] - 1), 0)

    def w_map(t, te, na):
        return (layer, te[jnp.minimum(t, na[0] - 1)], 0, 0)

    vmem = 2 * (3 * d * f * 4 + 2 * EXPERT_TILE * d * 4) + 3 * d * f * 2 + (6 << 20)
    return pl.pallas_call(
        _expert_kernel,
        grid_spec=pltpu.PrefetchScalarGridSpec(
            num_scalar_prefetch=2, grid=(n_tiles,),
            in_specs=[pl.BlockSpec((EXPERT_TILE, d), row_map),
                      pl.BlockSpec((None, None, d, f), w_map), pl.BlockSpec((None, None, d, f), w_map),
                      pl.BlockSpec((None, None, f, d), w_map)],
            out_specs=pl.BlockSpec((EXPERT_TILE, d), lambda t, te, na: (t, 0)),
            scratch_shapes=[pltpu.VMEM((d, f), BF16), pltpu.VMEM((d, f), BF16), pltpu.VMEM((f, d), BF16)]),
        out_shape=jax.ShapeDtypeStruct((p, d), F32),
        compiler_params=_cparams(("arbitrary",), vmem), name="moe_experts",
    )(tile_expert, n_active, x_sorted, w1, w3, w2)


def _combine_kernel(dest_ref, y_ref, x_ref, r_ref, g_ref, b_ref, o_ref, o2_ref, buf0, buf1, sem, *, tm,
                    prompt_tiles):
    t = pl.program_id(0)

    def copies(r):
        row = t * tm + r
        return (pltpu.make_async_copy(y_ref.at[pl.ds(dest_ref[2 * row], 1)], buf0.at[pl.ds(r, 1)], sem),
                pltpu.make_async_copy(y_ref.at[pl.ds(dest_ref[2 * row + 1], 1)], buf1.at[pl.ds(r, 1)], sem))

    def start(r, c):
        for cp in copies(r):
            cp.start()
        return c

    def wait(r, c):
        for cp in copies(r):
            cp.wait()
        return c

    lax.fori_loop(0, tm, start, 0)
    lax.fori_loop(0, tm, wait, 0)
    r = r_ref[...]
    ffn = r[:, 2:3] * buf0[...] + r[:, 3:4] * buf1[...]
    out = _layer_norm_rows(DEEPNORM_ALPHA * x_ref[...] + ffn, g_ref[...], b_ref[...])
    if prompt_tiles is None:
        o_ref[...] = out
        o2_ref[...] = out.astype(BF16)
    else:
        @pl.when(t < prompt_tiles)
        def _():
            o_ref[...] = out

        @pl.when(t >= prompt_tiles)
        def _():
            o2_ref[...] = out


def _combine_ln(y_sorted, dest_flat, x1, route, g, b, split_rows):
    n, d = x1.shape
    tm = 128
    vec = lambda t, dest: (0, 0)
    if split_rows is None:
        prompt_tiles = None
        out_specs = [pl.BlockSpec((tm, d), lambda t, dest: (t, 0)), pl.BlockSpec((tm, d), lambda t, dest: (t, 0))]
        out_shape = [jax.ShapeDtypeStruct((n, d), F32), jax.ShapeDtypeStruct((n, d), BF16)]
    else:
        prompt_tiles = split_rows // tm
        out_specs = [pl.BlockSpec((tm, d), lambda t, dest: (jnp.minimum(t, prompt_tiles - 1), 0)),
                     pl.BlockSpec((tm, d), lambda t, dest: (jnp.maximum(t - prompt_tiles, 0), 0))]
        out_shape = [jax.ShapeDtypeStruct((split_rows, d), F32), jax.ShapeDtypeStruct((n - split_rows, d), F32)]
    kern = functools.partial(_combine_kernel, tm=tm, prompt_tiles=prompt_tiles)
    return pl.pallas_call(
        kern,
        grid_spec=pltpu.PrefetchScalarGridSpec(
            num_scalar_prefetch=1, grid=(n // tm,),
            in_specs=[pl.BlockSpec(memory_space=pl.ANY),
                      pl.BlockSpec((tm, d), lambda t, dest: (t, 0)),
                      pl.BlockSpec((tm, ROUTER_W), lambda t, dest: (t, 0)),
                      pl.BlockSpec((1, d), vec), pl.BlockSpec((1, d), vec)],
            out_specs=out_specs,
            scratch_shapes=[pltpu.VMEM((tm, d), F32), pltpu.VMEM((tm, d), F32), pltpu.SemaphoreType.DMA(())]),
        out_shape=out_shape,
        compiler_params=_cparams(("arbitrary",), 40 << 20), name="moe_combine_ln2",
    )(dest_flat, y_sorted, x1, route, g, b)


def _route_plan(route, counts, n_tiles):
    ids = route[:, 0:2].astype(jnp.int32)
    rank = route[:, 4:6].astype(jnp.int32)
    cnt = counts[0, N_GROUPS:N_GROUPS + N_EXPERTS].astype(jnp.int32)
    tiles = (cnt + EXPERT_TILE - 1) // EXPERT_TILE
    tile_end = jnp.cumsum(tiles)
    tile_start = tile_end - tiles
    dest = (tile_start * EXPERT_TILE)[ids] + rank
    n_active = tile_end[-1:]
    tile_ids = jnp.arange(n_tiles, dtype=jnp.int32)
    tile_expert = jnp.minimum(jnp.sum((tile_end[None, :] <= tile_ids[:, None]).astype(jnp.int32), axis=1),
                              N_EXPERTS - 1)
    return dest.reshape(-1), tile_expert, n_active.astype(jnp.int32)


def _rope_tables(pos, d):
    inv = ROPE_BASE ** (-jnp.arange(0, d, 2, dtype=F32) / d)
    ang = pos.astype(F32)[:, None] * inv[None, :]
    c, s = jnp.cos(ang), jnp.sin(ang)
    cos = jnp.concatenate([c, c], axis=-1)
    sin = jnp.concatenate([-s, s], axis=-1)
    if d < LANES:
        cos = jnp.pad(cos, ((0, 0), (0, LANES - d)))
        sin = jnp.pad(sin, ((0, 0), (0, LANES - d)))
    return cos, sin


def _prep_weights(l, w_in, w_out, mla_w_uq, mla_w_uk, mla_w_uv, moe_w_group, moe_b_group, moe_w_expert,
                  moe_b_expert):
    d = w_in.shape[1]
    offs = [0]
    for s in (RET_HEADS * RET_DK, RET_HEADS * RET_DK, RET_HEADS * RET_DV, RET_HEADS * RET_DV, MLA_Q_LORA,
              MLA_KV_LORA, MLA_ROPE, 2 * CONV_CH):
        offs.append(offs[-1] + s)
    wi = w_in[l]
    seg = lambda i: wi[:, offs[i]:offs[i + 1]]
    w_in_p = jnp.concatenate(
        [seg(0), seg(1), seg(2), seg(3), seg(4), seg(6), jnp.zeros((d, H_KR_W - MLA_ROPE), F32), seg(5), seg(7)],
        axis=1).astype(BF16)
    uq = mla_w_uq[l]
    uq_p = jnp.pad(uq, ((0, 0), (0, 0), (0, Q_HEAD_W - MLA_NOPE - MLA_ROPE))).reshape(MLA_Q_LORA, -1).astype(BF16)
    uk = mla_w_uk[l].reshape(MLA_KV_LORA, MLA_HEADS * MLA_NOPE).astype(BF16)
    uv = mla_w_uv[l].reshape(MLA_KV_LORA, MLA_HEADS * MLA_V).astype(BF16)
    w_r = jnp.concatenate([moe_w_group[l], moe_w_expert[l].reshape(d, N_EXPERTS),
                           jnp.zeros((d, ROUTER_W - N_GROUPS - N_EXPERTS), F32)], axis=1)
    b_r = jnp.concatenate([moe_b_group[l], moe_b_expert[l].reshape(N_EXPERTS),
                           jnp.zeros((ROUTER_W - N_GROUPS - N_EXPERTS,), F32)])[None, :]
    return w_in_p, w_out[l].astype(BF16), uq_p, uk, uv, w_r, b_r


def kernel(x_prompt, x_sample, cache_kv_latent, cache_k_rope, state_ret, state_conv, page_table, w_in, w_out, ret_gn_g, ret_gn_b, mla_q_norm_g, mla_kv_norm_g, mla_w_uq, mla_w_uk, mla_w_uv, conv_w_dw, conv_b_dw, conv_gn_g, conv_gn_b, ln1_g, ln1_b, ln2_g, ln2_b, moe_w_group, moe_b_group, moe_w_expert, moe_b_expert, moe_w1, moe_w3, moe_w2):
    batch, seq, d_model = x_prompt.shape
    db, dec_seq, _ = x_sample.shape
    assert dec_seq == 1, "the sample group is a single-token decode step"
    depth = w_in.shape[0]
    n_prompt = batch * seq
    n_tok = n_prompt + db
    past = page_table.shape[1] * cache_kv_latent.shape[2]

    pos_p = jnp.arange(seq, dtype=jnp.int32)
    pos_s = past + jnp.arange(dec_seq, dtype=jnp.int32)
    pos_all = jnp.concatenate([jnp.tile(pos_p, batch), jnp.tile(pos_s, db)])
    cos64, sin64 = _rope_tables(pos_all, MLA_ROPE)
    cos128_p, sin128_p = _rope_tables(pos_p, RET_DK)
    cos128_s, sin128_s = _rope_tables(pos_s, RET_DK)

    n_slots = 2 * n_tok
    n_tiles = (n_slots + N_EXPERTS * (EXPERT_TILE - 1)) // EXPERT_TILE + 1

    n_phys, page = cache_kv_latent.shape[1], cache_kv_latent.shape[2]
    cache_kv = cache_kv_latent.reshape(depth * n_phys, page, MLA_KV_LORA)
    cache_kr_t = jnp.swapaxes(cache_k_rope, 2, 3).reshape(depth * n_phys, MLA_ROPE, page)
    state_conv_t = jnp.swapaxes(state_conv, 1, 2)

    x = jnp.concatenate([x_prompt.reshape(n_prompt, d_model), x_sample.reshape(db, d_model)], axis=0)
    xb = x.astype(BF16)
    outs = [[] for _ in range(8)]
    for l in range(depth):
        w_in_p, w_out_b, uq_p, uk, uv, w_r, b_r = _prep_weights(
            l, w_in, w_out, mla_w_uq, mla_w_uk, mla_w_uv, moe_w_group, moe_b_group, moe_w_expert, moe_b_expert)
        h = _matmul(xb, w_in_p, F32, 1024, "in_proj")

        cqn, lat, latb, krope = _mla_prep(h, cos64, sin64, mla_q_norm_g[l][None], mla_kv_norm_g[l][None])
        q_full = _qproj(cqn, uq_p, cos64, sin64)
        k_full, v_full = _kvup(latb, krope, uk, uv, n_prompt)
        m_mla = _attn_prompt(q_full, k_full, v_full, batch, seq, n_tok)
        qlat = _qlat(q_full, uk, n_prompt, db)
        hp = BF16_ROWS
        qlat = jnp.pad(jnp.swapaxes(qlat, 0, 1), ((0, 0), (0, hp - MLA_HEADS), (0, 0)))
        qr = q_full[n_prompt:].reshape(db, MLA_HEADS, Q_HEAD_W)[:, :, MLA_NOPE:MLA_NOPE + MLA_ROPE]
        qr = jnp.pad((qr.astype(F32) * MLA_SCALE).astype(BF16), ((0, 0), (0, hp - MLA_HEADS), (0, 0)))
        ctx = _attn_sample(page_table + l * n_phys, qlat, qr, lat[n_prompt:].reshape(db, 1, MLA_KV_LORA),
                           krope[n_prompt:].reshape(db, 1, LANES), cache_kv, cache_kr_t)
        m_mla = _ctxproj(jnp.swapaxes(ctx[:, :MLA_HEADS], 0, 1), uv, m_mla, n_prompt)

        m_ret, s_ret_p = _ret_prompt(h, cos128_p, sin128_p, ret_gn_g[l][None], ret_gn_b[l][None], batch, seq, n_tok)
        m_ret, s_ret_s = _ret_sample(h, cos128_s, sin128_s, ret_gn_g[l][None], ret_gn_b[l][None], state_ret, l,
                                     m_ret, n_prompt)

        m_conv, buf_p = _conv_prompt(h, conv_w_dw[l], conv_b_dw[l][None], conv_gn_g[l][None], conv_gn_b[l][None],
                                     batch, seq, n_tok)
        m_conv, buf_s = _conv_sample(h, conv_w_dw[l], conv_b_dw[l][None], conv_gn_g[l][None], conv_gn_b[l][None],
                                     state_conv_t, l, m_conv, n_prompt)

        z = _outproj(m_ret, m_mla, m_conv, w_out_b, x)
        x1, route, counts = _ln_router(z, ln1_g[l][None], ln1_b[l][None], w_r, b_r)

        dest, tile_expert, n_active = _route_plan(route, counts, n_tiles)
        x_sorted = _scatter_rows(x1, dest, n_tiles * EXPERT_TILE)
        y_sorted = _experts(x_sorted, tile_expert, n_active, moe_w1, moe_w3, moe_w2, l)
        x, xb = _combine_ln(y_sorted, dest, x1, route, ln2_g[l][None], ln2_b[l][None],
                            n_prompt if l == depth - 1 else None)

        outs[0].append(lat[:n_prompt].reshape(batch, seq, MLA_KV_LORA))
        outs[1].append(krope[:n_prompt, :MLA_ROPE].reshape(batch, seq, MLA_ROPE))
        outs[2].append(s_ret_p)
        outs[3].append(buf_p)
        outs[4].append(lat[n_prompt:].reshape(db, dec_seq, MLA_KV_LORA))
        outs[5].append(krope[n_prompt:, :MLA_ROPE].reshape(db, dec_seq, MLA_ROPE))
        outs[6].append(s_ret_s)
        outs[7].append(buf_s)

    y_prompt = x.reshape(batch, seq, d_model)
    y_sample = xb.reshape(db, dec_seq, d_model)
    stacked = [jnp.stack(o) for o in outs]
    stacked[7] = jnp.swapaxes(stacked[7], 1, 2)
    return (y_prompt, y_sample) + tuple(stacked)
```
